```python
import jax, jax.numpy as jnp
from jax import lax
import numpy as np

D_MODEL = 1024
BATCH = 2
SEQ = 8192
DEPTH = 1

HEAD_DIM = 64
SB_HEADS = 8
DIL_GROUPS = ((128, 1), (512, 4), (2048, 16))
DIL_KV_HEADS = 4
DIL_Q_HEADS = DIL_KV_HEADS * len(DIL_GROUPS)
D_FF = 2816
ROPE_THETA = 10000.0
RMS_EPS = 1e-6
Q_BLOCK = 128

SB_W = SB_HEADS * HEAD_DIM
DQ_W = DIL_Q_HEADS * HEAD_DIM
DKV_W = DIL_KV_HEADS * HEAD_DIM
IN_SIZES = (SB_W, SB_W, SB_W, DQ_W, DKV_W, DKV_W, 2 * D_MODEL)
IN_WIDTH = SB_W * 3 + DQ_W + 2 * DKV_W + 2 * D_MODEL

kernel_name = "hybrid_stickbreak_dilated_macaron"


def rmsnorm(x, g):
    xf = x.astype(jnp.float32)
    y = xf * lax.rsqrt(jnp.mean(xf * xf, axis=-1, keepdims=True) + RMS_EPS)
    return (y * g.astype(jnp.float32)).astype(x.dtype)


def swiglu(h, w1, w3, w2):
    return (jax.nn.silu(h @ w1) * (h @ w3)) @ w2


def split_heads(t, n):
    b, s, _ = t.shape
    return t.reshape(b, s, n, HEAD_DIM).transpose(0, 2, 1, 3)


def merge_heads(t):
    b, h, s, dh = t.shape
    return t.transpose(0, 2, 1, 3).reshape(b, s, h * dh)


def rope(t):
    s, dh = t.shape[2], t.shape[3]
    half = dh // 2
    inv_freq = ROPE_THETA ** (-jnp.arange(half, dtype=jnp.float32) / half)
    ang = jnp.arange(s, dtype=jnp.float32)[:, None] * inv_freq[None, :]
    cos, sin = jnp.cos(ang), jnp.sin(ang)
    tf = t.astype(jnp.float32)
    t1, t2 = tf[..., :half], tf[..., half:]
    out = jnp.concatenate([t1 * cos - t2 * sin, t2 * cos + t1 * sin], axis=-1)
    return out.astype(t.dtype)


def stick_breaking_attention(q, k, v):
    s_len = q.shape[2]
    scale = HEAD_DIM ** -0.5
    outs = []
    for i in range(s_len // Q_BLOCK):
        t0, end = i * Q_BLOCK, (i + 1) * Q_BLOCK
        qb, kb, vb = q[:, :, t0:end], k[:, :, :end], v[:, :, :end]
        z = jnp.einsum('bhqd,bhkd->bhqk', qb, kb).astype(jnp.float32) * scale
        t_pos = t0 + jnp.arange(Q_BLOCK)[:, None]
        s_pos = jnp.arange(end)[None, :]
        strict = s_pos < t_pos
        log_not = jnp.where(strict, jax.nn.log_sigmoid(-z), 0.0)
        between = lax.cumsum(log_not, axis=3, reverse=True) - log_not
        a = jnp.where(strict, jnp.exp(jax.nn.log_sigmoid(z) + between), 0.0)
        outs.append(jnp.einsum('bhqk,bhkd->bhqd', a.astype(vb.dtype), vb))
    return jnp.concatenate(outs, axis=2)


def dilated_window_attention(q, k, v):
    b, _, s_len, dh = q.shape
    n_groups = len(DIL_GROUPS)
    nb = s_len // Q_BLOCK
    scale = HEAD_DIM ** -0.5
    qg = q.reshape(b, n_groups, DIL_KV_HEADS, nb, Q_BLOCK, dh).transpose(3, 1, 0, 2, 4, 5)

    def block(args):
        qblk, t0 = args
        t_pos = t0 + jnp.arange(Q_BLOCK)
        outs, lses = [], []
        for g, (window, dil) in enumerate(DIL_GROUPS):
            m = jnp.arange(window // dil + 1)
            idx = t_pos[:, None] - dil * m[None, :]
            valid = idx >= 0
            idx = jnp.maximum(idx, 0)
            kg = jnp.take(k, idx, axis=2)
            vg = jnp.take(v, idx, axis=2)
            sc = jnp.einsum('bhqd,bhqmd->bhqm', qblk[g], kg).astype(jnp.float32) * scale
            sc = jnp.where(valid, sc, -jnp.inf)
            mx = jnp.max(sc, axis=-1, keepdims=True)
            e = jnp.exp(sc - mx)
            den = jnp.sum(e, axis=-1, keepdims=True)
            p = e / den
            outs.append(jnp.einsum('bhqm,bhqmd->bhqd', p.astype(vg.dtype), vg).astype(jnp.float32))
            lses.append(mx + jnp.log(den))
        alpha = jax.nn.softmax(jnp.stack(lses), axis=0)
        return jnp.sum(alpha * jnp.stack(outs), axis=0).astype(v.dtype)

    t0s = jnp.arange(nb, dtype=jnp.int32) * Q_BLOCK
    o = lax.map(block, (qg, t0s))
    return o.transpose(1, 2, 0, 3, 4).reshape(b, DIL_KV_HEADS, s_len, dh)


def setup_inputs(seed: int = 0) -> dict:
    key = jax.random.key(seed)
    ks = jax.random.split(key, 20)

    def w(k, shape, fan_in):
        return jax.random.normal(k, shape, jnp.float32) * fan_in ** -0.5

    def gain(k, shape):
        return 1.0 + 0.05 * jax.random.normal(k, shape, jnp.float32)

    L = DEPTH
    return {
        "x": jax.random.normal(ks[0], (BATCH, SEQ, D_MODEL), jnp.float32),
        "g_ffn1": gain(ks[1], (L, D_MODEL)),
        "w1_a": w(ks[2], (L, D_MODEL, D_FF), D_MODEL),
        "w3_a": w(ks[3], (L, D_MODEL, D_FF), D_MODEL),
        "w2_a": w(ks[4], (L, D_FF, D_MODEL), D_FF),
        "g_mix": gain(ks[5], (L, D_MODEL)),
        "w_in": w(ks[6], (L, D_MODEL, IN_WIDTH), D_MODEL),
        "b_gate": 0.01 * jax.random.normal(ks[7], (L, 2 * D_MODEL), jnp.float32),
        "w_sb_out": w(ks[8], (L, SB_W, D_MODEL), SB_W),
        "w_dil_out": w(ks[9], (L, DKV_W, D_MODEL), DKV_W),
        "w_o": w(ks[10], (L, D_MODEL, D_MODEL), D_MODEL),
        "g_ffn2": gain(ks[11], (L, D_MODEL)),
        "w1_b": w(ks[12], (L, D_MODEL, D_FF), D_MODEL),
        "w3_b": w(ks[13], (L, D_MODEL, D_FF), D_MODEL),
        "w2_b": w(ks[14], (L, D_FF, D_MODEL), D_FF),
        "g_final": gain(ks[15], (D_MODEL,)),
    }


def reference(x, g_ffn1, w1_a, w3_a, w2_a, g_mix, w_in, b_gate, w_sb_out, w_dil_out, w_o,
              g_ffn2, w1_b, w3_b, w2_b, g_final):
    split_points = [int(p) for p in np.cumsum(IN_SIZES)[:-1]]
    for l in range(DEPTH):
        h = rmsnorm(x, g_ffn1[l])
        x = x + 0.5 * swiglu(h, w1_a[l], w3_a[l], w2_a[l])

        h = rmsnorm(x, g_mix[l])
        proj = h @ w_in[l]
        q_sb, k_sb, v_sb, q_dl, k_dl, v_dl, gate_pre = jnp.split(proj, split_points, axis=-1)
        gates = jax.nn.sigmoid(gate_pre + b_gate[l])
        g_sb, g_dl = gates[..., :D_MODEL], gates[..., D_MODEL:]

        y_sb = stick_breaking_attention(split_heads(q_sb, SB_HEADS), split_heads(k_sb, SB_HEADS),
                                        split_heads(v_sb, SB_HEADS))
        y_sb = merge_heads(y_sb) @ w_sb_out[l]

        y_dl = dilated_window_attention(rope(split_heads(q_dl, DIL_Q_HEADS)),
                                        rope(split_heads(k_dl, DIL_KV_HEADS)),
                                        split_heads(v_dl, DIL_KV_HEADS))
        y_dl = merge_heads(y_dl) @ w_dil_out[l]

        x = x + (g_sb * y_sb + g_dl * y_dl) @ w_o[l]

        h = rmsnorm(x, g_ffn2[l])
        x = x + 0.5 * swiglu(h, w1_b[l], w3_b[l], w2_b[l])
    return rmsnorm(x, g_final)
```

```python
import functools

import jax
import jax.numpy as jnp
from jax import lax
from jax.experimental import pallas as pl
from jax.experimental.pallas import tpu as pltpu

D_MODEL = 1024
HEAD_DIM = 64
SB_HEADS = 8
DIL_GROUPS = ((128, 1), (512, 4), (2048, 16))
DIL_KV_HEADS = 4
D_FF = 2816
ROPE_THETA = 10000.0
RMS_EPS = 1e-6

SB_W = SB_HEADS * HEAD_DIM
DKV_W = DIL_KV_HEADS * HEAD_DIM
DQ_W = DKV_W * len(DIL_GROUPS)
QKV_W = 3 * SB_W + DQ_W + 2 * DKV_W
SCALE = HEAD_DIM ** -0.5

LANES = 128
TOKEN_TILE = 512
FF_CHUNK = 256
SB_BLOCK = 128
DIL_ROWS = 512
DIL_BAND = 128
SB_LOG_FLOOR = -88.0
NEG_BIG = -1e30
VMEM_LIMIT = 48 * 1024 * 1024

F32 = jnp.float32
BF16 = jnp.bfloat16


def _rmsnorm(x, g):
    ms = jnp.mean(x * x, axis=-1, keepdims=True)
    return x * lax.rsqrt(ms + RMS_EPS) * g


def _resident(shape):
    return pl.BlockSpec(shape, lambda *_: (0,) * len(shape), pipeline_mode=pl.Buffered(1))


def _params(semantics):
    return pltpu.CompilerParams(dimension_semantics=semantics, vmem_limit_bytes=VMEM_LIMIT)


def _ffn_kernel(*refs, final_norm):
    if final_norm:
        x_ref, g_ref, w1_ref, w3_ref, w2_ref, gf_ref, o_ref, acc_ref = refs
    else:
        x_ref, g_ref, w1_ref, w3_ref, w2_ref, o_ref, acc_ref = refs
    x = x_ref[...]
    h = _rmsnorm(x, g_ref[...]).astype(BF16)
    for c in range(D_FF // FF_CHUNK):
        sl = slice(c * FF_CHUNK, (c + 1) * FF_CHUNK)
        u = jnp.dot(h, w1_ref[:, sl], preferred_element_type=F32)
        v = jnp.dot(h, w3_ref[:, sl], preferred_element_type=F32)
        act = (u * jax.nn.sigmoid(u) * v).astype(BF16)
        part = jnp.dot(act, w2_ref[sl, :], preferred_element_type=F32)
        if c == 0:
            acc_ref[...] = part
        else:
            acc_ref[...] += part
    y = x + 0.5 * acc_ref[...]
    if final_norm:
        y = _rmsnorm(y, gf_ref[...])
    o_ref[...] = y


def _ffn(x, g, w1, w3, w2, g_final=None):
    n = x.shape[0]
    tile = pl.BlockSpec((TOKEN_TILE, D_MODEL), lambda i: (i, 0))
    in_specs = [tile, _resident((1, D_MODEL)), _resident((D_MODEL, D_FF)),
                _resident((D_MODEL, D_FF)), _resident((D_FF, D_MODEL))]
    args = [x, g, w1, w3, w2]
    if g_final is not None:
        in_specs.append(_resident((1, D_MODEL)))
        args.append(g_final)
    return pl.pallas_call(
        functools.partial(_ffn_kernel, final_norm=g_final is not None),
        out_shape=jax.ShapeDtypeStruct((n, D_MODEL), F32),
        grid=(n // TOKEN_TILE,),
        in_specs=in_specs,
        out_specs=tile,
        scratch_shapes=[pltpu.VMEM((TOKEN_TILE, D_MODEL), F32)],
        compiler_params=_params(("parallel",)),
        name="ffn_final" if g_final is not None else "ffn",
    )(*args)


def _rope_cols(t, cos, sin_signed):
    lane = lax.broadcasted_iota(jnp.int32, t.shape, 1)
    up = pltpu.roll(t, HEAD_DIM // 2, 1)
    down = pltpu.roll(t, LANES - HEAD_DIM // 2, 1)
    partner = jnp.where((lane & (HEAD_DIM // 2)) == 0, down, up)
    return t * cos + partner * sin_signed


def _proj_kernel(x_ref, g_ref, w_ref, cos_ref, sin_ref,
                 qs_ref, ks_ref, vs_ref, qd_ref, kd_ref, vd_ref):
    h = _rmsnorm(x_ref[...], g_ref[...]).astype(BF16)
    cos = cos_ref[...]
    sin = sin_ref[...]

    def cols(lo, width):
        return jnp.dot(h, w_ref[:, lo:lo + width], preferred_element_type=F32)

    qs_ref[...] = (cols(0, SB_W) * SCALE).astype(BF16)
    ks_ref[...] = cols(SB_W, SB_W).astype(BF16)
    vs_ref[...] = cols(2 * SB_W, SB_W).astype(BF16)
    base = 3 * SB_W
    for c in range(DQ_W // LANES):
        t = cols(base + c * LANES, LANES)
        qd_ref[:, c * LANES:(c + 1) * LANES] = (_rope_cols(t, cos, sin) * SCALE).astype(BF16)
    base += DQ_W
    for c in range(DKV_W // LANES):
        t = cols(base + c * LANES, LANES)
        kd_ref[:, c * LANES:(c + 1) * LANES] = _rope_cols(t, cos, sin).astype(BF16)
    base += DKV_W
    vd_ref[...] = cols(base, DKV_W).astype(BF16)


def _proj(x, g, w_qkv, cos, sin, seq):
    n = x.shape[0]
    pos_blocks = seq // TOKEN_TILE

    def tile(width):
        return pl.BlockSpec((TOKEN_TILE, width), lambda i: (i, 0))

    table = pl.BlockSpec((TOKEN_TILE, LANES), lambda i: (i % pos_blocks, 0))
    widths = (SB_W, SB_W, SB_W, DQ_W, DKV_W, DKV_W)
    return pl.pallas_call(
        _proj_kernel,
        out_shape=[jax.ShapeDtypeStruct((n, w), BF16) for w in widths],
        grid=(n // TOKEN_TILE,),
        in_specs=[tile(D_MODEL), _resident((1, D_MODEL)), _resident((D_MODEL, QKV_W)), table, table],
        out_specs=[tile(w) for w in widths],
        compiler_params=_params(("parallel",)),
        name="proj",
    )(x, g, w_qkv, cos, sin)


def _sb_kernel(q_ref, k_ref, v_ref, tri_ref, o_ref, acc_ref, carry_ref):
    qi = pl.program_id(2)
    q = q_ref[...]
    lane = lax.broadcasted_iota(jnp.int32, q.shape, 1)
    zero = jnp.zeros_like(q)
    q_heads = (jnp.where(lane < HEAD_DIM, q, zero), jnp.where(lane >= HEAD_DIM, q, zero))
    row = lax.broadcasted_iota(jnp.int32, (SB_BLOCK, SB_BLOCK), 0) + qi * SB_BLOCK
    col = lax.broadcasted_iota(jnp.int32, (SB_BLOCK, SB_BLOCK), 1)
    tri = tri_ref[...]

    acc_ref[...] = jnp.zeros_like(acc_ref)
    carry_ref[...] = jnp.zeros_like(carry_ref)

    def visit(state):
        kb, _ = state
        start = pl.multiple_of(kb * SB_BLOCK, SB_BLOCK)
        kblk = k_ref[pl.ds(start, SB_BLOCK), :]
        vblk = v_ref[pl.ds(start, SB_BLOCK), :]
        strict = (col + kb * SB_BLOCK) < row
        worst = None
        for hd in range(2):
            z = lax.dot_general(q_heads[hd], kblk, (((1,), (1,)), ((), ())),
                                preferred_element_type=F32)
            log_not = -(jnp.maximum(z, 0.0) + jnp.log(1.0 + jnp.exp(-jnp.abs(z))))
            log_beta = z + log_not
            log_not = jnp.where(strict, log_not, 0.0)
            p0 = log_not.astype(BF16)
            r1 = log_not - p0.astype(F32)
            p1 = r1.astype(BF16)
            p2 = (r1 - p1.astype(F32)).astype(BF16)
            sums = (jnp.dot(p0, tri, preferred_element_type=F32)
                    + jnp.dot(p1, tri, preferred_element_type=F32)
                    + jnp.dot(p2, tri, preferred_element_type=F32))
            between = sums[:, :SB_BLOCK]
            total = sums[:, SB_BLOCK:]
            carry = carry_ref[hd]
            a = jnp.where(strict, jnp.exp(log_beta + between + carry), 0.0)
            acc_ref[hd] += jnp.dot(a.astype(BF16), vblk, preferred_element_type=F32)
            carry = carry + total
            carry_ref[hd] = carry
            top = jnp.max(carry)
            worst = top if worst is None else jnp.maximum(worst, top)
        return kb - 1, (worst > SB_LOG_FLOOR).astype(jnp.int32)

    def live(state):
        kb, go = state
        return jnp.logical_and(kb >= 0, go > 0)

    lax.while_loop(live, visit, (qi, jnp.int32(1)))
    o_ref[...] = jnp.where(lane < HEAD_DIM, acc_ref[0], acc_ref[1]).astype(o_ref.dtype)


def _sb_attention(q, k, v, batch, seq):
    q3, k3, v3 = (t.reshape(batch, seq, SB_W) for t in (q, k, v))
    j = lax.broadcasted_iota(jnp.int32, (SB_BLOCK, SB_BLOCK), 0)
    s = lax.broadcasted_iota(jnp.int32, (SB_BLOCK, SB_BLOCK), 1)
    tri = jnp.concatenate([(j > s).astype(BF16), jnp.ones((SB_BLOCK, LANES), BF16)], axis=1)
    qblk = pl.BlockSpec((None, SB_BLOCK, LANES), lambda b, p, i: (b, i, p))
    whole = pl.BlockSpec((None, seq, LANES), lambda b, p, i: (b, 0, p))
    out = pl.pallas_call(
        _sb_kernel,
        out_shape=jax.ShapeDtypeStruct((batch, seq, SB_W), BF16),
        grid=(batch, SB_W // LANES, seq // SB_BLOCK),
        in_specs=[qblk, whole, whole, _resident((SB_BLOCK, SB_BLOCK + LANES))],
        out_specs=qblk,
        scratch_shapes=[pltpu.VMEM((2, SB_BLOCK, LANES), F32), pltpu.VMEM((2, SB_BLOCK, LANES), F32)],
        compiler_params=_params(("parallel", "parallel", "arbitrary")),
        name="sb_attn",
    )(q3, k3, v3, tri)
    return out.reshape(batch * seq, SB_W)


def _dil_kernel(q_ref, kprev_ref, k_ref, vprev_ref, v_ref, o_ref, lse_ref):
    step = pl.program_id(2)
    sub_blocks = DIL_ROWS // DIL_BAND
    row = lax.broadcasted_iota(jnp.int32, (DIL_BAND, 2 * DIL_BAND), 0)
    col = lax.broadcasted_iota(jnp.int32, (DIL_BAND, 2 * DIL_BAND), 1)
    band = jnp.logical_and(col >= row, col <= row + DIL_BAND)
    lane = lax.broadcasted_iota(jnp.int32, (DIL_BAND, LANES), 1)
    low = lane < HEAD_DIM
    for j in range(sub_blocks):
        rows = slice(j * DIL_BAND, (j + 1) * DIL_BAND)
        if j == 0:
            valid = jnp.logical_and(band, jnp.logical_or(col >= DIL_BAND, step > 0))
            kp, vp = kprev_ref[...], vprev_ref[...]
        else:
            valid = band
            prev = slice((j - 1) * DIL_BAND, j * DIL_BAND)
            kp, vp = k_ref[prev, :], v_ref[prev, :]
        kcat = jnp.concatenate([kp, k_ref[rows, :]], axis=0)
        vcat = jnp.concatenate([vp, v_ref[rows, :]], axis=0)
        for pair in range(DKV_W // LANES):
            lanes = slice(pair * LANES, (pair + 1) * LANES)
            q = q_ref[rows, lanes]
            kk = kcat[:, lanes]
            vv = vcat[:, lanes]
            outs, lses = [], []
            for hd in range(2):
                qh = jnp.where(low if hd == 0 else jnp.logical_not(low), q, jnp.zeros_like(q))
                s = lax.dot_general(qh, kk, (((1,), (1,)), ((), ())), preferred_element_type=F32)
                s = jnp.where(valid, s, NEG_BIG)
                mx = jnp.max(s, axis=-1, keepdims=True)
                e = jnp.exp(s - mx)
                den = jnp.sum(e, axis=-1, keepdims=True)
                o = jnp.dot(e.astype(BF16), vv, preferred_element_type=F32)
                outs.append(o / den)
                lses.append(jnp.broadcast_to(mx + jnp.log(den), (DIL_BAND, LANES)))
            o_ref[rows, lanes] = jnp.where(low, outs[0], outs[1])
            lse_ref[rows, lanes] = jnp.where(low, lses[0], lses[1])


def _dil_attention(q, k, v, group, dil, batch, seq):
    sub_len = seq // dil
    rows = min(DIL_ROWS, sub_len)
    assert rows == DIL_ROWS
    n_groups = len(DIL_GROUPS)
    qv = q.reshape(batch, sub_len, dil * DQ_W)
    kv = k.reshape(batch, sub_len, dil * DKV_W)
    vv = v.reshape(batch, sub_len, dil * DKV_W)
    per = rows // DIL_BAND
    q_spec = pl.BlockSpec((None, rows, DKV_W), lambda b, r, i: (b, i, r * n_groups + group))
    cur = pl.BlockSpec((None, rows, DKV_W), lambda b, r, i: (b, i, r))
    prev = pl.BlockSpec((None, DIL_BAND, DKV_W), lambda b, r, i: (b, jnp.maximum(i * per - 1, 0), r))
    out, lse = pl.pallas_call(
        _dil_kernel,
        out_shape=[jax.ShapeDtypeStruct((batch, sub_len, dil * DKV_W), F32)] * 2,
        grid=(batch, dil, sub_len // rows),
        in_specs=[q_spec, prev, cur, prev, cur],
        out_specs=[cur, cur],
        compiler_params=_params(("parallel", "parallel", "arbitrary")),
        name=f"dil_attn_d{dil}",
    )(qv, kv, kv, vv, vv)
    return out.reshape(batch * seq, DKV_W), lse.reshape(batch * seq, DKV_W)


def _mix_kernel(x_ref, g_ref, wg_ref, bg_ref, osb_ref, o0_ref, o1_ref, o2_ref,
                l0_ref, l1_ref, l2_ref, wsb_ref, wdl_ref, wo_ref, out_ref):
    x = x_ref[...]
    h = _rmsnorm(x, g_ref[...]).astype(BF16)
    gates = jax.nn.sigmoid(jnp.dot(h, wg_ref[...], preferred_element_type=F32) + bg_ref[...])
    lses = (l0_ref[...], l1_ref[...], l2_ref[...])
    outs = (o0_ref[...], o1_ref[...], o2_ref[...])
    top = jnp.maximum(jnp.maximum(lses[0], lses[1]), lses[2])
    ws = [jnp.exp(l - top) for l in lses]
    o_dl = (ws[0] * outs[0] + ws[1] * outs[1] + ws[2] * outs[2]) / (ws[0] + ws[1] + ws[2])
    y_sb = jnp.dot(osb_ref[...], wsb_ref[...], preferred_element_type=F32)
    y_dl = jnp.dot(o_dl.astype(BF16), wdl_ref[...], preferred_element_type=F32)
    mixed = gates[:, :D_MODEL] * y_sb + gates[:, D_MODEL:] * y_dl
    out_ref[...] = x + jnp.dot(mixed.astype(BF16), wo_ref[...], preferred_element_type=F32)


def _mix(x, g, w_gate, b_gate, o_sb, dil_outs, dil_lses, w_sb_out, w_dil_out, w_o):
    n = x.shape[0]

    def tile(width):
        return pl.BlockSpec((TOKEN_TILE, width), lambda i: (i, 0))

    return pl.pallas_call(
        _mix_kernel,
        out_shape=jax.ShapeDtypeStruct((n, D_MODEL), F32),
        grid=(n // TOKEN_TILE,),
        in_specs=[tile(D_MODEL), _resident((1, D_MODEL)), _resident((D_MODEL, 2 * D_MODEL)),
                  _resident((1, 2 * D_MODEL)), tile(SB_W)] + [tile(DKV_W)] * 6
                 + [_resident((SB_W, D_MODEL)), _resident((DKV_W, D_MODEL)), _resident((D_MODEL, D_MODEL))],
        out_specs=tile(D_MODEL),
        compiler_params=_params(("parallel",)),
        name="mix",
    )(x, g, w_gate, b_gate, o_sb, *dil_outs, *dil_lses, w_sb_out, w_dil_out, w_o)


def _rope_tables(seq):
    half = HEAD_DIM // 2
    inv_freq = ROPE_THETA ** (-jnp.arange(half, dtype=F32) / half)
    ang = jnp.arange(seq, dtype=F32)[:, None] * inv_freq[None, :]
    cos, sin = jnp.cos(ang), jnp.sin(ang)
    reps = LANES // HEAD_DIM
    return jnp.tile(jnp.concatenate([cos, cos], axis=1), (1, reps)), \
        jnp.tile(jnp.concatenate([-sin, sin], axis=1), (1, reps))


def kernel(x, g_ffn1, w1_a, w3_a, w2_a, g_mix, w_in, b_gate, w_sb_out, w_dil_out, w_o,
           g_ffn2, w1_b, w3_b, w2_b, g_final):
    batch, seq, _ = x.shape
    depth = g_ffn1.shape[0]
    assert seq % DIL_ROWS == 0 and all(seq // d >= DIL_ROWS for _, d in DIL_GROUPS)
    assert all(w == DIL_BAND * d for w, d in DIL_GROUPS)
    cos, sin = _rope_tables(seq)
    row = lambda t: t.reshape(1, -1)
    bf = lambda t: t.astype(BF16)
    xt = x.reshape(batch * seq, D_MODEL)
    for l in range(depth):
        xt = _ffn(xt, row(g_ffn1[l]), bf(w1_a[l]), bf(w3_a[l]), bf(w2_a[l]))
        w_in_l = bf(w_in[l])
        q_sb, k_sb, v_sb, q_dl, k_dl, v_dl = _proj(xt, row(g_mix[l]), w_in_l[:, :QKV_W], cos, sin, seq)
        o_sb = _sb_attention(q_sb, k_sb, v_sb, batch, seq)
        dil = [_dil_attention(q_dl, k_dl, v_dl, g, d, batch, seq) for g, (_, d) in enumerate(DIL_GROUPS)]
        xt = _mix(xt, row(g_mix[l]), w_in_l[:, QKV_W:], row(b_gate[l]), o_sb,
                  [o for o, _ in dil], [s for _, s in dil], bf(w_sb_out[l]), bf(w_dil_out[l]), bf(w_o[l]))
        last = l == depth - 1
        xt = _ffn(xt, row(g_ffn2[l]), bf(w1_b[l]), bf(w3_b[l]), bf(w2_b[l]),
                  row(g_final) if last else None)
    return xt.reshape(batch, seq, D_MODEL)
```

```python
import functools

import jax
import jax.numpy as jnp
from jax import lax
from jax.experimental import pallas as pl
from jax.experimental.pallas import tpu as pltpu

D_MODEL = 1024
HEAD_DIM = 64
SB_HEADS = 8
DIL_GROUPS = ((128, 1), (512, 4), (2048, 16))
DIL_KV_HEADS = 4
D_FF = 2816
ROPE_THETA = 10000.0
RMS_EPS = 1e-6

SB_W = SB_HEADS * HEAD_DIM
DKV_W = DIL_KV_HEADS * HEAD_DIM
DQ_W = DKV_W * len(DIL_GROUPS)
QKV_W = 3 * SB_W + DQ_W + 2 * DKV_W
SCALE = HEAD_DIM ** -0.5

LANES = 128
TOKEN_TILE = 512
FF_CHUNK = 256
SB_BLOCK = 128
SB_GROUP = 4
DIL_BAND = 128
DIL_WINDOW = DIL_BAND * max(d for _, d in DIL_GROUPS)
DIL_BATCH = 4
SB_LOG_FLOOR = -88.0
NEG_BIG = -1e30
VMEM_LIMIT = 48 * 1024 * 1024

F32 = jnp.float32
BF16 = jnp.bfloat16
_NT = (((1,), (1,)), ((), ()))


def _rmsnorm(x, g):
    ms = jnp.mean(x * x, axis=-1, keepdims=True)
    return x * lax.rsqrt(ms + RMS_EPS) * g


def _resident(shape):
    return pl.BlockSpec(shape, lambda *_: (0,) * len(shape), pipeline_mode=pl.Buffered(1))


def _params(semantics):
    return pltpu.CompilerParams(dimension_semantics=semantics, vmem_limit_bytes=VMEM_LIMIT)


def _ffn_kernel(*refs, final_norm):
    if final_norm:
        x_ref, g_ref, w1_ref, w3_ref, w2_ref, gf_ref, o_ref, acc_ref = refs
    else:
        x_ref, g_ref, w1_ref, w3_ref, w2_ref, o_ref, acc_ref = refs
    x = x_ref[...]
    h = _rmsnorm(x, g_ref[...]).astype(BF16)
    for c in range(D_FF // FF_CHUNK):
        sl = slice(c * FF_CHUNK, (c + 1) * FF_CHUNK)
        u = jnp.dot(h, w1_ref[:, sl], preferred_element_type=F32)
        v = jnp.dot(h, w3_ref[:, sl], preferred_element_type=F32)
        act = (u * jax.nn.sigmoid(u) * v).astype(BF16)
        part = jnp.dot(act, w2_ref[sl, :], preferred_element_type=F32)
        if c == 0:
            acc_ref[...] = part
        else:
            acc_ref[...] += part
    y = x + 0.5 * acc_ref[...]
    if final_norm:
        y = _rmsnorm(y, gf_ref[...])
    o_ref[...] = y


def _ffn(x, g, w1, w3, w2, g_final=None):
    n = x.shape[0]
    tile = pl.BlockSpec((TOKEN_TILE, D_MODEL), lambda i: (i, 0))
    in_specs = [tile, _resident((1, D_MODEL)), _resident((D_MODEL, D_FF)),
                _resident((D_MODEL, D_FF)), _resident((D_FF, D_MODEL))]
    args = [x, g, w1, w3, w2]
    if g_final is not None:
        in_specs.append(_resident((1, D_MODEL)))
        args.append(g_final)
    return pl.pallas_call(
        functools.partial(_ffn_kernel, final_norm=g_final is not None),
        out_shape=jax.ShapeDtypeStruct((n, D_MODEL), F32),
        grid=(n // TOKEN_TILE,),
        in_specs=in_specs,
        out_specs=tile,
        scratch_shapes=[pltpu.VMEM((TOKEN_TILE, D_MODEL), F32)],
        compiler_params=_params(("parallel",)),
        name="ffn_final" if g_final is not None else "ffn",
    )(*args)


def _rope_cols(t, cos, sin_signed):
    lane = lax.broadcasted_iota(jnp.int32, t.shape, 1)
    up = pltpu.roll(t, HEAD_DIM // 2, 1)
    down = pltpu.roll(t, LANES - HEAD_DIM // 2, 1)
    partner = jnp.where((lane & (HEAD_DIM // 2)) == 0, down, up)
    return t * cos + partner * sin_signed


def _proj_kernel(x_ref, g_ref, w_ref, cos_ref, sin_ref,
                 qs_ref, ks_ref, vs_ref, qd_ref, kd_ref, vd_ref):
    h = _rmsnorm(x_ref[...], g_ref[...]).astype(BF16)
    cos = cos_ref[...]
    sin = sin_ref[...]

    def cols(lo, width):
        return jnp.dot(h, w_ref[:, lo:lo + width], preferred_element_type=F32)

    qs_ref[...] = (cols(0, SB_W) * SCALE).astype(BF16)
    ks_ref[...] = cols(SB_W, SB_W).astype(BF16)
    vs_ref[...] = cols(2 * SB_W, SB_W).astype(BF16)
    base = 3 * SB_W
    for c in range(DQ_W // LANES):
        t = cols(base + c * LANES, LANES)
        qd_ref[:, c * LANES:(c + 1) * LANES] = _rope_cols(t, cos, sin) * SCALE
    base += DQ_W
    for c in range(DKV_W // LANES):
        t = cols(base + c * LANES, LANES)
        kd_ref[:, c * LANES:(c + 1) * LANES] = _rope_cols(t, cos, sin)
    base += DKV_W
    vd_ref[...] = cols(base, DKV_W)


def _proj(x, g, w_qkv, cos, sin, seq):
    n = x.shape[0]
    pos_blocks = seq // TOKEN_TILE

    def tile(width):
        return pl.BlockSpec((TOKEN_TILE, width), lambda i: (i, 0))

    table = pl.BlockSpec((TOKEN_TILE, LANES), lambda i: (i % pos_blocks, 0))
    outs = ((SB_W, BF16), (SB_W, BF16), (SB_W, BF16), (DQ_W, F32), (DKV_W, F32), (DKV_W, F32))
    return pl.pallas_call(
        _proj_kernel,
        out_shape=[jax.ShapeDtypeStruct((n, w), dt) for w, dt in outs],
        grid=(n // TOKEN_TILE,),
        in_specs=[tile(D_MODEL), _resident((1, D_MODEL)), _resident((D_MODEL, QKV_W)), table, table],
        out_specs=[tile(w) for w, _ in outs],
        compiler_params=_params(("parallel",)),
        name="proj",
    )(x, g, w_qkv, cos, sin)


def _split3(x):
    p0 = x.astype(BF16)
    r1 = x - p0.astype(F32)
    p1 = r1.astype(BF16)
    p2 = (r1 - p1.astype(F32)).astype(BF16)
    return jnp.concatenate([p0, p1, p2], axis=1)


def _log_terms(z):
    log_not = -(jnp.maximum(z, 0.0) + jnp.log(1.0 + jnp.exp(-jnp.abs(z))))
    return log_not, z + log_not


def _sb_kernel(q_ref, k_ref, v_ref, tri2_ref, tri_ref, o_ref, acc_ref, carry_ref):
    step = pl.program_id(2)
    lane = lax.broadcasted_iota(jnp.int32, (SB_BLOCK, LANES), 1)
    head_lanes = (lane < HEAD_DIM, lane >= HEAD_DIM)
    row = lax.broadcasted_iota(jnp.int32, (SB_BLOCK, 2 * SB_BLOCK), 0)
    col = lax.broadcasted_iota(jnp.int32, (SB_BLOCK, 2 * SB_BLOCK), 1)

    older, starts, stricts = [], [], []
    for g in range(SB_GROUP):
        blk = step * SB_GROUP + g
        first = jnp.maximum(blk - 1, 0)
        starts.append(pl.multiple_of(first * SB_BLOCK, SB_BLOCK))
        stricts.append((col + first * SB_BLOCK) < (row + blk * SB_BLOCK))
        older.append(first - 1)
    pairs = [(g, hd) for g in range(SB_GROUP) for hd in range(2)]
    zs = []
    for g, hd in pairs:
        q = q_ref[g * SB_BLOCK:(g + 1) * SB_BLOCK, :]
        qh = jnp.where(head_lanes[hd], q, jnp.zeros_like(q))
        kwin = k_ref[pl.ds(starts[g], 2 * SB_BLOCK), :]
        zs.append(lax.dot_general(qh, kwin, _NT, preferred_element_type=F32))
    log_betas, pieces, tops = [], [], []
    for (g, hd), z in zip(pairs, zs):
        log_not, log_beta = _log_terms(z)
        log_not = jnp.where(stricts[g], log_not, 0.0)
        log_betas.append(log_beta)
        pieces.append(_split3(log_not))
        total = jnp.sum(log_not, axis=1, keepdims=True)
        carry_ref[g, hd] = jnp.broadcast_to(total, (SB_BLOCK, LANES))
        tops.append(jnp.max(total))
    betweens = [jnp.dot(p, tri2_ref[...], preferred_element_type=F32) for p in pieces]
    weights = [jnp.where(stricts[g], jnp.exp(lb + bt), 0.0).astype(BF16)
               for (g, _), lb, bt in zip(pairs, log_betas, betweens)]
    for (g, hd), a in zip(pairs, weights):
        vwin = v_ref[pl.ds(starts[g], 2 * SB_BLOCK), :]
        acc_ref[g, hd] = jnp.dot(a, vwin, preferred_element_type=F32)
    go = [(jnp.maximum(tops[2 * g], tops[2 * g + 1]) > SB_LOG_FLOOR).astype(jnp.int32)
          for g in range(SB_GROUP)]

    for g in range(SB_GROUP):

        def visit(state, g=g):
            kb, _ = state
            q = q_ref[g * SB_BLOCK:(g + 1) * SB_BLOCK, :]
            at = pl.multiple_of(kb * SB_BLOCK, SB_BLOCK)
            kblk = k_ref[pl.ds(at, SB_BLOCK), :]
            vblk = v_ref[pl.ds(at, SB_BLOCK), :]
            worst = None
            for hd in range(2):
                qh = jnp.where(head_lanes[hd], q, jnp.zeros_like(q))
                z = lax.dot_general(qh, kblk, _NT, preferred_element_type=F32)
                log_not, log_beta = _log_terms(z)
                sums = jnp.dot(_split3(log_not), tri_ref[...], preferred_element_type=F32)
                carry = carry_ref[g, hd]
                a = jnp.exp(log_beta + sums[:, :SB_BLOCK] + carry)
                acc_ref[g, hd] += jnp.dot(a.astype(BF16), vblk, preferred_element_type=F32)
                carry = carry + sums[:, SB_BLOCK:]
                carry_ref[g, hd] = carry
                top = jnp.max(carry)
                worst = top if worst is None else jnp.maximum(worst, top)
            return kb - 1, (worst > SB_LOG_FLOOR).astype(jnp.int32)

        def live(state):
            kb, more = state
            return jnp.logical_and(kb >= 0, more > 0)

        lax.while_loop(live, visit, (older[g], go[g]))
        o_ref[g * SB_BLOCK:(g + 1) * SB_BLOCK, :] = jnp.where(
            head_lanes[0], acc_ref[g, 0], acc_ref[g, 1]).astype(o_ref.dtype)


def _suffix_matrix(n):
    j = lax.broadcasted_iota(jnp.int32, (n, n), 0)
    s = lax.broadcasted_iota(jnp.int32, (n, n), 1)
    return (j > s).astype(BF16)


def _sb_attention(q, k, v, batch, seq):
    q3, k3, v3 = (t.reshape(batch, seq, SB_W) for t in (q, k, v))
    tri2 = jnp.tile(_suffix_matrix(2 * SB_BLOCK), (3, 1))
    tri = jnp.tile(jnp.concatenate([_suffix_matrix(SB_BLOCK), jnp.ones((SB_BLOCK, LANES), BF16)], axis=1), (3, 1))
    rows = SB_GROUP * SB_BLOCK
    qblk = pl.BlockSpec((None, rows, LANES), lambda b, p, i: (b, i, p))
    whole = pl.BlockSpec((None, seq, LANES), lambda b, p, i: (b, 0, p))
    scratch = pltpu.VMEM((SB_GROUP, 2, SB_BLOCK, LANES), F32)
    out = pl.pallas_call(
        _sb_kernel,
        out_shape=jax.ShapeDtypeStruct((batch, seq, SB_W), BF16),
        grid=(batch, SB_W // LANES, seq // rows),
        in_specs=[qblk, whole, whole, _resident(tri2.shape), _resident(tri.shape)],
        out_specs=qblk,
        scratch_shapes=[scratch, scratch],
        compiler_params=_params(("parallel", "parallel", "arbitrary")),
        name="sb_attn",
    )(q3, k3, v3, tri2, tri)
    return out.reshape(batch * seq, SB_W)


def _strided(start, size, stride):
    return pl.ds(start, size) if stride == 1 else pl.ds(start, size, stride=stride)


def _band_attention(blocks, head_lanes):
    scores = []
    for q, kcat, _, _ in blocks:
        for hd in range(2):
            qh = jnp.where(head_lanes[hd], q, jnp.zeros_like(q))
            scores.append(lax.dot_general(qh, kcat, _NT, preferred_element_type=F32))
    probs, stats = [], []
    for i, s in enumerate(scores):
        s = jnp.where(blocks[i // 2][3], s, NEG_BIG)
        mx = jnp.max(s, axis=-1, keepdims=True)
        e = jnp.exp(s - mx)
        probs.append(e.astype(BF16))
        stats.append((mx, jnp.sum(e, axis=-1, keepdims=True)))
    results = []
    for b, (_, _, vcat, _) in enumerate(blocks):
        outs, lses = [], []
        for hd in range(2):
            mx, den = stats[2 * b + hd]
            o = jnp.dot(probs[2 * b + hd], vcat, preferred_element_type=F32)
            outs.append(o / den)
            lses.append(jnp.broadcast_to(mx + jnp.log(den), (DIL_BAND, LANES)))
        results.append((jnp.where(head_lanes[0], outs[0], outs[1]),
                        jnp.where(head_lanes[0], lses[0], lses[1])))
    return results


def _dil_kernel(q0_ref, q1_ref, q2_ref, kprev_ref, k_ref, vprev_ref, v_ref, o_ref, obuf, lbuf):
    step = pl.program_id(2)
    row = lax.broadcasted_iota(jnp.int32, (DIL_BAND, 2 * DIL_BAND), 0)
    col = lax.broadcasted_iota(jnp.int32, (DIL_BAND, 2 * DIL_BAND), 1)
    band = jnp.logical_and(col >= row, col <= row + DIL_BAND)
    band_first = jnp.logical_and(band, jnp.logical_or(col >= DIL_BAND, step > 0))
    lane = lax.broadcasted_iota(jnp.int32, (DIL_BAND, LANES), 1)
    head_lanes = (lane < HEAD_DIM, lane >= HEAD_DIM)
    q_refs = (q0_ref, q1_ref, q2_ref)

    def load(g, dil, r, j):
        base = r + dil * DIL_BAND * j
        if dil == 1 and not isinstance(j, int):
            base = pl.multiple_of(base, DIL_BAND)
        q = q_refs[g][_strided(base, DIL_BAND, dil), :].astype(BF16)
        if isinstance(j, int) and j == 0:
            tail = DIL_WINDOW - dil * DIL_BAND + r
            kcat = jnp.concatenate([kprev_ref[_strided(tail, DIL_BAND, dil), :],
                                    k_ref[_strided(r, DIL_BAND, dil), :]], axis=0)
            vcat = jnp.concatenate([vprev_ref[_strided(tail, DIL_BAND, dil), :],
                                    v_ref[_strided(r, DIL_BAND, dil), :]], axis=0)
            valid = band_first
        else:
            kcat = k_ref[_strided(base - dil * DIL_BAND, 2 * DIL_BAND, dil), :]
            vcat = v_ref[_strided(base - dil * DIL_BAND, 2 * DIL_BAND, dil), :]
            valid = band
        return base, (q, kcat.astype(BF16), vcat.astype(BF16), valid)

    def run(g, dil, where):
        loaded = [load(g, dil, r, j) for r, j in where]
        results = _band_attention([blk for _, blk in loaded], head_lanes)
        for (base, _), (out, lse) in zip(loaded, results):
            obuf[g, _strided(base, DIL_BAND, dil), :] = out
            lbuf[g, _strided(base, DIL_BAND, dil), :] = lse

    for g, (_, dil) in enumerate(DIL_GROUPS):
        per_residue = DIL_WINDOW // (dil * DIL_BAND)
        if dil == 1:
            run(g, dil, [(0, j) for j in range(DIL_BATCH)])

            def body(i, _, g=g, dil=dil):
                run(g, dil, [(0, i * DIL_BATCH + j) for j in range(DIL_BATCH)])
                return 0

            lax.fori_loop(1, per_residue // DIL_BATCH, body, 0)
        elif per_residue >= DIL_BATCH:
            def body(r, _, g=g, dil=dil, per_residue=per_residue):
                for j0 in range(0, per_residue, DIL_BATCH):
                    run(g, dil, [(r, j0 + j) for j in range(DIL_BATCH)])
                return 0

            lax.fori_loop(0, dil, body, 0)
        else:
            per_trip = DIL_BATCH // per_residue

            def body(i, _, g=g, dil=dil, per_residue=per_residue, per_trip=per_trip):
                run(g, dil, [(i * per_trip + r, j) for r in range(per_trip) for j in range(per_residue)])
                return 0

            lax.fori_loop(0, dil // per_trip, body, 0)

    chunk = 2 * DIL_BAND
    for c in range(DIL_WINDOW // chunk):
        rows = slice(c * chunk, (c + 1) * chunk)
        lses = [lbuf[g, rows, :] for g in range(len(DIL_GROUPS))]
        top = functools.reduce(jnp.maximum, lses)
        ws = [jnp.exp(l - top) for l in lses]
        num = sum(w * obuf[g, rows, :] for g, w in enumerate(ws))
        o_ref[rows, :] = (num / sum(ws)).astype(o_ref.dtype)


def _dil_attention(q, k, v, batch, seq):
    n_groups = len(DIL_GROUPS)
    pairs = DKV_W // LANES
    q3 = q.reshape(batch, seq, DQ_W)
    k3 = k.reshape(batch, seq, DKV_W)
    v3 = v.reshape(batch, seq, DKV_W)

    def q_spec(g):
        return pl.BlockSpec((None, DIL_WINDOW, LANES), lambda b, p, i: (b, i, g * pairs + p))

    cur = pl.BlockSpec((None, DIL_WINDOW, LANES), lambda b, p, i: (b, i, p))
    prev = pl.BlockSpec((None, DIL_WINDOW, LANES), lambda b, p, i: (b, jnp.maximum(i - 1, 0), p))
    buf = pltpu.VMEM((n_groups, DIL_WINDOW, LANES), F32)
    out = pl.pallas_call(
        _dil_kernel,
        out_shape=jax.ShapeDtypeStruct((batch, seq, DKV_W), BF16),
        grid=(batch, pairs, seq // DIL_WINDOW),
        in_specs=[q_spec(g) for g in range(n_groups)] + [prev, cur, prev, cur],
        out_specs=cur,
        scratch_shapes=[buf, buf],
        compiler_params=_params(("parallel", "parallel", "arbitrary")),
        name="dil_attn",
    )(q3, q3, q3, k3, k3, v3, v3)
    return out.reshape(batch * seq, DKV_W)


def _mix_kernel(x_ref, g_ref, wg_ref, bg_ref, osb_ref, odl_ref, wsb_ref, wdl_ref, wo_ref, out_ref):
    x = x_ref[...]
    h = _rmsnorm(x, g_ref[...]).astype(BF16)
    gates = jax.nn.sigmoid(jnp.dot(h, wg_ref[...], preferred_element_type=F32) + bg_ref[...])
    y_sb = jnp.dot(osb_ref[...], wsb_ref[...], preferred_element_type=F32)
    y_dl = jnp.dot(odl_ref[...], wdl_ref[...], preferred_element_type=F32)
    mixed = gates[:, :D_MODEL] * y_sb + gates[:, D_MODEL:] * y_dl
    out_ref[...] = x + jnp.dot(mixed.astype(BF16), wo_ref[...], preferred_element_type=F32)


def _mix(x, g, w_gate, b_gate, o_sb, o_dl, w_sb_out, w_dil_out, w_o):
    n = x.shape[0]

    def tile(width):
        return pl.BlockSpec((TOKEN_TILE, width), lambda i: (i, 0))

    return pl.pallas_call(
        _mix_kernel,
        out_shape=jax.ShapeDtypeStruct((n, D_MODEL), F32),
        grid=(n // TOKEN_TILE,),
        in_specs=[tile(D_MODEL), _resident((1, D_MODEL)), _resident((D_MODEL, 2 * D_MODEL)),
                  _resident((1, 2 * D_MODEL)), tile(SB_W), tile(DKV_W),
                  _resident((SB_W, D_MODEL)), _resident((DKV_W, D_MODEL)), _resident((D_MODEL, D_MODEL))],
        out_specs=tile(D_MODEL),
        compiler_params=_params(("parallel",)),
        name="mix",
    )(x, g, w_gate, b_gate, o_sb, o_dl, w_sb_out, w_dil_out, w_o)


def _rope_tables(seq):
    half = HEAD_DIM // 2
    inv_freq = ROPE_THETA ** (-jnp.arange(half, dtype=F32) / half)
    ang = jnp.arange(seq, dtype=F32)[:, None] * inv_freq[None, :]
    cos, sin = jnp.cos(ang), jnp.sin(ang)
    reps = LANES // HEAD_DIM
    return jnp.tile(jnp.concatenate([cos, cos], axis=1), (1, reps)), \
        jnp.tile(jnp.concatenate([-sin, sin], axis=1), (1, reps))


def kernel(x, g_ffn1, w1_a, w3_a, w2_a, g_mix, w_in, b_gate, w_sb_out, w_dil_out, w_o,
           g_ffn2, w1_b, w3_b, w2_b, g_final):
    batch, seq, _ = x.shape
    depth = g_ffn1.shape[0]
    assert seq % DIL_WINDOW == 0 and seq % (SB_GROUP * SB_BLOCK) == 0 and seq % TOKEN_TILE == 0
    assert all(w == DIL_BAND * d and DIL_WINDOW % (DIL_BAND * d) == 0 for w, d in DIL_GROUPS)
    cos, sin = _rope_tables(seq)
    row = lambda t: t.reshape(1, -1)
    bf = lambda t: t.astype(BF16)
    xt = x.reshape(batch * seq, D_MODEL)
    for l in range(depth):
        xt = _ffn(xt, row(g_ffn1[l]), bf(w1_a[l]), bf(w3_a[l]), bf(w2_a[l]))
        w_in_l = bf(w_in[l])
        q_sb, k_sb, v_sb, q_dl, k_dl, v_dl = _proj(xt, row(g_mix[l]), w_in_l[:, :QKV_W], cos, sin, seq)
        o_sb = _sb_attention(q_sb, k_sb, v_sb, batch, seq)
        o_dl = _dil_attention(q_dl, k_dl, v_dl, batch, seq)
        xt = _mix(xt, row(g_mix[l]), w_in_l[:, QKV_W:], row(b_gate[l]), o_sb, o_dl,
                  bf(w_sb_out[l]), bf(w_dil_out[l]), bf(w_o[l]))
        last = l == depth - 1
        xt = _ffn(xt, row(g_ffn2[l]), bf(w1_b[l]), bf(w3_b[l]), bf(w2_b[l]),
                  row(g_final) if last else None)
    return xt.reshape(batch, seq, D_MODEL)
```

```python
import functools

import jax
import jax.numpy as jnp
from jax import lax
from jax.experimental import pallas as pl
from jax.experimental.pallas import tpu as pltpu

D_MODEL = 1024
HEAD_DIM = 64
SB_HEADS = 8
DIL_GROUPS = ((128, 1), (512, 4), (2048, 16))
DIL_KV_HEADS = 4
D_FF = 2816
ROPE_THETA = 10000.0
RMS_EPS = 1e-6

SB_W = SB_HEADS * HEAD_DIM
DKV_W = DIL_KV_HEADS * HEAD_DIM
DQ_W = DKV_W * len(DIL_GROUPS)
QKV_W = 3 * SB_W + DQ_W + 2 * DKV_W
SCALE = HEAD_DIM ** -0.5

LANES = 128
TOKEN_TILE = 512
FF_CHUNK = 256
SB_BLOCK = 128
SB_GROUP = 8
DIL_BAND = 128
DIL_WINDOW = DIL_BAND * max(d for _, d in DIL_GROUPS)
DIL_BATCH = 4
SB_LOG_FLOOR = -88.0
NEG_BIG = -1e30
VMEM_LIMIT = 56 * 1024 * 1024

F32 = jnp.float32
BF16 = jnp.bfloat16
_NT = (((1,), (1,)), ((), ()))


def _rmsnorm(x, g):
    ms = jnp.mean(x * x, axis=-1, keepdims=True)
    return x * lax.rsqrt(ms + RMS_EPS) * g


def _resident(shape):
    return pl.BlockSpec(shape, lambda *_: (0,) * len(shape), pipeline_mode=pl.Buffered(1))


def _params(semantics):
    return pltpu.CompilerParams(dimension_semantics=semantics, vmem_limit_bytes=VMEM_LIMIT)


def _swiglu_half_step(x, g, w1_ref, w3_ref, w2_ref, acc_ref):
    h = _rmsnorm(x, g).astype(BF16)
    for c in range(D_FF // FF_CHUNK):
        sl = slice(c * FF_CHUNK, (c + 1) * FF_CHUNK)
        u = jnp.dot(h, w1_ref[:, sl], preferred_element_type=F32)
        v = jnp.dot(h, w3_ref[:, sl], preferred_element_type=F32)
        act = (u * jax.nn.sigmoid(u) * v).astype(BF16)
        part = jnp.dot(act, w2_ref[sl, :], preferred_element_type=F32)
        if c == 0:
            acc_ref[...] = part
        else:
            acc_ref[...] += part
    return x + 0.5 * acc_ref[...]


def _ffn_weight_specs():
    return [_resident((D_MODEL, D_FF)), _resident((D_MODEL, D_FF)), _resident((D_FF, D_MODEL))]


def _token_tile(width):
    return pl.BlockSpec((TOKEN_TILE, width), lambda i: (i, 0))


def _rope_cols(t, cos, sin_signed):
    lane = lax.broadcasted_iota(jnp.int32, t.shape, 1)
    up = pltpu.roll(t, HEAD_DIM // 2, 1)
    down = pltpu.roll(t, LANES - HEAD_DIM // 2, 1)
    partner = jnp.where((lane & (HEAD_DIM // 2)) == 0, down, up)
    return t * cos + partner * sin_signed


def _ffn_proj_kernel(x_ref, g1_ref, w1_ref, w3_ref, w2_ref, gm_ref, w_ref, cos_ref, sin_ref,
                     x1_ref, qs_ref, ks_ref, vs_ref, qd_ref, kd_ref, vd_ref, acc_ref):
    x1 = _swiglu_half_step(x_ref[...], g1_ref[...], w1_ref, w3_ref, w2_ref, acc_ref)
    x1_ref[...] = x1
    h = _rmsnorm(x1, gm_ref[...]).astype(BF16)
    cos = cos_ref[...]
    sin = sin_ref[...]

    def cols(lo, width):
        return jnp.dot(h, w_ref[:, lo:lo + width], preferred_element_type=F32)

    qs_ref[...] = (cols(0, SB_W) * SCALE).astype(BF16)
    ks_ref[...] = cols(SB_W, SB_W).astype(BF16)
    vs_ref[...] = cols(2 * SB_W, SB_W).astype(BF16)
    base = 3 * SB_W
    qd = cols(base, DQ_W)
    for c in range(DQ_W // LANES):
        sl = slice(c * LANES, (c + 1) * LANES)
        qd_ref[:, sl] = _rope_cols(qd[:, sl], cos, sin) * SCALE
    base += DQ_W
    kd = cols(base, DKV_W)
    for c in range(DKV_W // LANES):
        sl = slice(c * LANES, (c + 1) * LANES)
        kd_ref[:, sl] = _rope_cols(kd[:, sl], cos, sin)
    base += DKV_W
    vd_ref[...] = cols(base, DKV_W)


def _ffn_proj(x, g1, w1, w3, w2, g_mix, w_qkv, cos, sin, seq):
    n = x.shape[0]
    pos_blocks = seq // TOKEN_TILE
    table = pl.BlockSpec((TOKEN_TILE, LANES), lambda i: (i % pos_blocks, 0))
    outs = ((D_MODEL, F32), (SB_W, BF16), (SB_W, BF16), (SB_W, BF16), (DQ_W, F32), (DKV_W, F32), (DKV_W, F32))
    return pl.pallas_call(
        _ffn_proj_kernel,
        out_shape=[jax.ShapeDtypeStruct((n, w), dt) for w, dt in outs],
        grid=(n // TOKEN_TILE,),
        in_specs=[_token_tile(D_MODEL), _resident((1, D_MODEL))] + _ffn_weight_specs()
                 + [_resident((1, D_MODEL)), _resident((D_MODEL, QKV_W)), table, table],
        out_specs=[_token_tile(w) for w, _ in outs],
        scratch_shapes=[pltpu.VMEM((TOKEN_TILE, D_MODEL), F32)],
        compiler_params=_params(("parallel",)),
        name="ffn_proj",
    )(x, g1, w1, w3, w2, g_mix, w_qkv, cos, sin)


def _split2(x):
    p0 = x.astype(BF16)
    r1 = x - p0.astype(F32)
    p1 = r1.astype(BF16)
    return jnp.concatenate([p0, p1], axis=1)


def _log_terms(z):
    log_not = -(jnp.maximum(z, 0.0) + jnp.log(1.0 + jnp.exp(-jnp.abs(z))))
    return log_not, z + log_not


def _sb_kernel(q_ref, k_ref, v_ref, tri2_ref, tri_ref, o_ref, acc_ref, carry_ref):
    step = pl.program_id(2)
    lane = lax.broadcasted_iota(jnp.int32, (SB_BLOCK, LANES), 1)
    head_lanes = (lane < HEAD_DIM, lane >= HEAD_DIM)
    row = lax.broadcasted_iota(jnp.int32, (SB_BLOCK, 2 * SB_BLOCK), 0)
    col = lax.broadcasted_iota(jnp.int32, (SB_BLOCK, 2 * SB_BLOCK), 1)

    older, starts, stricts = [], [], []
    for g in range(SB_GROUP):
        blk = step * SB_GROUP + g
        first = jnp.maximum(blk - 1, 0)
        starts.append(pl.multiple_of(first * SB_BLOCK, SB_BLOCK))
        stricts.append((col + first * SB_BLOCK) < (row + blk * SB_BLOCK))
        older.append(first - 1)
    pairs = [(g, hd) for g in range(SB_GROUP) for hd in range(2)]
    zs = []
    for g, hd in pairs:
        q = q_ref[g * SB_BLOCK:(g + 1) * SB_BLOCK, :]
        qh = jnp.where(head_lanes[hd], q, jnp.zeros_like(q))
        kwin = k_ref[pl.ds(starts[g], 2 * SB_BLOCK), :]
        zs.append(lax.dot_general(qh, kwin, _NT, preferred_element_type=F32))
    log_betas, pieces, tops = [], [], []
    for (g, hd), z in zip(pairs, zs):
        log_not, log_beta = _log_terms(jnp.where(stricts[g], z, NEG_BIG))
        log_betas.append(log_beta)
        pieces.append(_split2(log_not))
        total = jnp.sum(log_not, axis=1, keepdims=True)
        carry_ref[g, hd] = jnp.broadcast_to(total, (SB_BLOCK, LANES))
        tops.append(jnp.max(total))
    betweens = [jnp.dot(p, tri2_ref[...], preferred_element_type=F32) for p in pieces]
    weights = [jnp.exp(lb + bt).astype(BF16) for lb, bt in zip(log_betas, betweens)]
    for (g, hd), a in zip(pairs, weights):
        vwin = v_ref[pl.ds(starts[g], 2 * SB_BLOCK), :]
        acc_ref[g, hd] = jnp.dot(a, vwin, preferred_element_type=F32)
    go = [(jnp.maximum(tops[2 * g], tops[2 * g + 1]) > SB_LOG_FLOOR).astype(jnp.int32)
          for g in range(SB_GROUP)]

    for g in range(SB_GROUP):

        def visit(state, g=g):
            kb, _ = state
            q = q_ref[g * SB_BLOCK:(g + 1) * SB_BLOCK, :]
            at = pl.multiple_of(kb * SB_BLOCK, SB_BLOCK)
            kblk = k_ref[pl.ds(at, SB_BLOCK), :]
            vblk = v_ref[pl.ds(at, SB_BLOCK), :]
            worst = None
            for hd in range(2):
                qh = jnp.where(head_lanes[hd], q, jnp.zeros_like(q))
                z = lax.dot_general(qh, kblk, _NT, preferred_element_type=F32)
                log_not, log_beta = _log_terms(z)
                sums = jnp.dot(_split2(log_not), tri_ref[...], preferred_element_type=F32)
                carry = carry_ref[g, hd]
                a = jnp.exp(log_beta + sums[:, :SB_BLOCK] + carry)
                acc_ref[g, hd] += jnp.dot(a.astype(BF16), vblk, preferred_element_type=F32)
                carry = carry + sums[:, SB_BLOCK:]
                carry_ref[g, hd] = carry
                top = jnp.max(carry)
                worst = top if worst is None else jnp.maximum(worst, top)
            return kb - 1, (worst > SB_LOG_FLOOR).astype(jnp.int32)

        def live(state):
            kb, more = state
            return jnp.logical_and(kb >= 0, more > 0)

        lax.while_loop(live, visit, (older[g], go[g]))
        o_ref[g * SB_BLOCK:(g + 1) * SB_BLOCK, :] = jnp.where(
            head_lanes[0], acc_ref[g, 0], acc_ref[g, 1]).astype(o_ref.dtype)


def _suffix_matrix(n):
    j = lax.broadcasted_iota(jnp.int32, (n, n), 0)
    s = lax.broadcasted_iota(jnp.int32, (n, n), 1)
    return (j > s).astype(BF16)


def _sb_attention(q, k, v, batch, seq):
    q3, k3, v3 = (t.reshape(batch, seq, SB_W) for t in (q, k, v))
    tri2 = jnp.tile(_suffix_matrix(2 * SB_BLOCK), (2, 1))
    tri = jnp.tile(jnp.concatenate([_suffix_matrix(SB_BLOCK), jnp.ones((SB_BLOCK, LANES), BF16)], axis=1), (2, 1))
    rows = SB_GROUP * SB_BLOCK
    qblk = pl.BlockSpec((None, rows, LANES), lambda b, p, i: (b, i, p))
    whole = pl.BlockSpec((None, seq, LANES), lambda b, p, i: (b, 0, p))
    scratch = pltpu.VMEM((SB_GROUP, 2, SB_BLOCK, LANES), F32)
    out = pl.pallas_call(
        _sb_kernel,
        out_shape=jax.ShapeDtypeStruct((batch, seq, SB_W), BF16),
        grid=(batch, SB_W // LANES, seq // rows),
        in_specs=[qblk, whole, whole, _resident(tri2.shape), _resident(tri.shape)],
        out_specs=qblk,
        scratch_shapes=[scratch, scratch],
        compiler_params=_params(("parallel", "parallel", "arbitrary")),
        name="sb_attn",
    )(q3, k3, v3, tri2, tri)
    return out.reshape(batch * seq, SB_W)


def _strided(start, size, stride):
    return pl.ds(start, size) if stride == 1 else pl.ds(start, size, stride=stride)


def _band_attention(blocks, head_lanes):
    scores = []
    for q, kcat, _, _ in blocks:
        for hd in range(2):
            qh = jnp.where(head_lanes[hd], q, jnp.zeros_like(q))
            scores.append(lax.dot_general(qh, kcat, _NT, preferred_element_type=F32))
    probs, stats = [], []
    for i, s in enumerate(scores):
        s = jnp.where(blocks[i // 2][3], s, NEG_BIG)
        mx = jnp.max(s, axis=-1, keepdims=True)
        e = jnp.exp(s - mx)
        probs.append(e.astype(BF16))
        stats.append((mx, jnp.sum(e, axis=-1, keepdims=True)))
    results = []
    for b, (_, _, vcat, _) in enumerate(blocks):
        outs, lses = [], []
        for hd in range(2):
            mx, den = stats[2 * b + hd]
            o = jnp.dot(probs[2 * b + hd], vcat, preferred_element_type=F32)
            outs.append(o / den)
            lses.append(jnp.broadcast_to(mx + jnp.log(den), (DIL_BAND, LANES)))
        results.append((jnp.where(head_lanes[0], outs[0], outs[1]),
                        jnp.where(head_lanes[0], lses[0], lses[1])))
    return results


def _dil_kernel(q0_ref, q1_ref, q2_ref, kprev_ref, k_ref, vprev_ref, v_ref, o_ref, obuf, lbuf):
    step = pl.program_id(2)
    row = lax.broadcasted_iota(jnp.int32, (DIL_BAND, 2 * DIL_BAND), 0)
    col = lax.broadcasted_iota(jnp.int32, (DIL_BAND, 2 * DIL_BAND), 1)
    band = jnp.logical_and(col >= row, col <= row + DIL_BAND)
    band_first = jnp.logical_and(band, jnp.logical_or(col >= DIL_BAND, step > 0))
    lane = lax.broadcasted_iota(jnp.int32, (DIL_BAND, LANES), 1)
    head_lanes = (lane < HEAD_DIM, lane >= HEAD_DIM)
    q_refs = (q0_ref, q1_ref, q2_ref)

    def load(g, dil, r, j):
        base = r + dil * DIL_BAND * j
        if dil == 1 and not isinstance(j, int):
            base = pl.multiple_of(base, DIL_BAND)
        q = q_refs[g][_strided(base, DIL_BAND, dil), :].astype(BF16)
        if isinstance(j, int) and j == 0:
            tail = DIL_WINDOW - dil * DIL_BAND + r
            kcat = jnp.concatenate([kprev_ref[_strided(tail, DIL_BAND, dil), :],
                                    k_ref[_strided(r, DIL_BAND, dil), :]], axis=0)
            vcat = jnp.concatenate([vprev_ref[_strided(tail, DIL_BAND, dil), :],
                                    v_ref[_strided(r, DIL_BAND, dil), :]], axis=0)
            valid = band_first
        else:
            kcat = k_ref[_strided(base - dil * DIL_BAND, 2 * DIL_BAND, dil), :]
            vcat = v_ref[_strided(base - dil * DIL_BAND, 2 * DIL_BAND, dil), :]
            valid = band
        return base, (q, kcat.astype(BF16), vcat.astype(BF16), valid)

    def run(g, dil, where):
        loaded = [load(g, dil, r, j) for r, j in where]
        results = _band_attention([blk for _, blk in loaded], head_lanes)
        for (base, _), (out, lse) in zip(loaded, results):
            obuf[g, _strided(base, DIL_BAND, dil), :] = out
            lbuf[g, _strided(base, DIL_BAND, dil), :] = lse

    for g, (_, dil) in enumerate(DIL_GROUPS):
        per_residue = DIL_WINDOW // (dil * DIL_BAND)
        if dil == 1:
            run(g, dil, [(0, j) for j in range(DIL_BATCH)])

            def body(i, _, g=g, dil=dil):
                run(g, dil, [(0, i * DIL_BATCH + j) for j in range(DIL_BATCH)])
                return 0

            lax.fori_loop(1, per_residue // DIL_BATCH, body, 0)
        elif per_residue >= DIL_BATCH:
            def body(r, _, g=g, dil=dil, per_residue=per_residue):
                for j0 in range(0, per_residue, DIL_BATCH):
                    run(g, dil, [(r, j0 + j) for j in range(DIL_BATCH)])
                return 0

            lax.fori_loop(0, dil, body, 0)
        else:
            per_trip = DIL_BATCH // per_residue

            def body(i, _, g=g, dil=dil, per_residue=per_residue, per_trip=per_trip):
                run(g, dil, [(i * per_trip + r, j) for r in range(per_trip) for j in range(per_residue)])
                return 0

            lax.fori_loop(0, dil // per_trip, body, 0)

    chunk = 2 * DIL_BAND
    for c in range(DIL_WINDOW // chunk):
        rows = slice(c * chunk, (c + 1) * chunk)
        lses = [lbuf[g, rows, :] for g in range(len(DIL_GROUPS))]
        top = functools.reduce(jnp.maximum, lses)
        ws = [jnp.exp(l - top) for l in lses]
        num = sum(w * obuf[g, rows, :] for g, w in enumerate(ws))
        o_ref[rows, :] = (num / sum(ws)).astype(o_ref.dtype)


def _dil_attention(q, k, v, batch, seq):
    n_groups = len(DIL_GROUPS)
    pairs = DKV_W // LANES
    q3 = q.reshape(batch, seq, DQ_W)
    k3 = k.reshape(batch, seq, DKV_W)
    v3 = v.reshape(batch, seq, DKV_W)

    def q_spec(g):
        return pl.BlockSpec((None, DIL_WINDOW, LANES), lambda b, p, i: (b, i, g * pairs + p))

    cur = pl.BlockSpec((None, DIL_WINDOW, LANES), lambda b, p, i: (b, i, p))
    prev = pl.BlockSpec((None, DIL_WINDOW, LANES), lambda b, p, i: (b, jnp.maximum(i - 1, 0), p))
    buf = pltpu.VMEM((n_groups, DIL_WINDOW, LANES), F32)
    out = pl.pallas_call(
        _dil_kernel,
        out_shape=jax.ShapeDtypeStruct((batch, seq, DKV_W), BF16),
        grid=(batch, pairs, seq // DIL_WINDOW),
        in_specs=[q_spec(g) for g in range(n_groups)] + [prev, cur, prev, cur],
        out_specs=cur,
        scratch_shapes=[buf, buf],
        compiler_params=_params(("parallel", "parallel", "arbitrary")),
        name="dil_attn",
    )(q3, q3, q3, k3, k3, v3, v3)
    return out.reshape(batch * seq, DKV_W)


def _mix_ffn_kernel(*refs, final_norm):
    (x_ref, gm_ref, wg_ref, bg_ref, osb_ref, odl_ref, wsb_ref, wdl_ref, wo_ref,
     g2_ref, w1_ref, w3_ref, w2_ref) = refs[:13]
    gf_ref = refs[13] if final_norm else None
    o_ref, acc_ref = refs[-2:]
    x = x_ref[...]
    h = _rmsnorm(x, gm_ref[...]).astype(BF16)

    def gate(lo):
        pre = jnp.dot(h, wg_ref[:, lo:lo + D_MODEL], preferred_element_type=F32)
        return jax.nn.sigmoid(pre + bg_ref[:, lo:lo + D_MODEL])

    mixed = gate(0) * jnp.dot(osb_ref[...], wsb_ref[...], preferred_element_type=F32)
    mixed = mixed + gate(D_MODEL) * jnp.dot(odl_ref[...], wdl_ref[...], preferred_element_type=F32)
    x2 = x + jnp.dot(mixed.astype(BF16), wo_ref[...], preferred_element_type=F32)
    y = _swiglu_half_step(x2, g2_ref[...], w1_ref, w3_ref, w2_ref, acc_ref)
    if final_norm:
        y = _rmsnorm(y, gf_ref[...])
    o_ref[...] = y


def _mix_ffn(x, g_mix, w_gate, b_gate, o_sb, o_dl, w_sb_out, w_dil_out, w_o, g2, w1, w3, w2, g_final=None):
    n = x.shape[0]
    in_specs = ([_token_tile(D_MODEL), _resident((1, D_MODEL)), _resident((D_MODEL, 2 * D_MODEL)),
                 _resident((1, 2 * D_MODEL)), _token_tile(SB_W), _token_tile(DKV_W),
                 _resident((SB_W, D_MODEL)), _resident((DKV_W, D_MODEL)), _resident((D_MODEL, D_MODEL)),
                 _resident((1, D_MODEL))] + _ffn_weight_specs())
    args = [x, g_mix, w_gate, b_gate, o_sb, o_dl, w_sb_out, w_dil_out, w_o, g2, w1, w3, w2]
    if g_final is not None:
        in_specs.append(_resident((1, D_MODEL)))
        args.append(g_final)
    return pl.pallas_call(
        functools.partial(_mix_ffn_kernel, final_norm=g_final is not None),
        out_shape=jax.ShapeDtypeStruct((n, D_MODEL), F32),
        grid=(n // TOKEN_TILE,),
        in_specs=in_specs,
        out_specs=_token_tile(D_MODEL),
        scratch_shapes=[pltpu.VMEM((TOKEN_TILE, D_MODEL), F32)],
        compiler_params=_params(("parallel",)),
        name="mix_ffn",
    )(*args)


def _rope_tables(seq):
    half = HEAD_DIM // 2
    inv_freq = ROPE_THETA ** (-jnp.arange(half, dtype=F32) / half)
    ang = jnp.arange(seq, dtype=F32)[:, None] * inv_freq[None, :]
    cos, sin = jnp.cos(ang), jnp.sin(ang)
    reps = LANES // HEAD_DIM
    return jnp.tile(jnp.concatenate([cos, cos], axis=1), (1, reps)), \
        jnp.tile(jnp.concatenate([-sin, sin], axis=1), (1, reps))


def kernel(x, g_ffn1, w1_a, w3_a, w2_a, g_mix, w_in, b_gate, w_sb_out, w_dil_out, w_o,
           g_ffn2, w1_b, w3_b, w2_b, g_final):
    batch, seq, _ = x.shape
    depth = g_ffn1.shape[0]
    assert seq % DIL_WINDOW == 0 and seq % (SB_GROUP * SB_BLOCK) == 0 and seq % TOKEN_TILE == 0
    assert all(w == DIL_BAND * d and DIL_WINDOW % (DIL_BAND * d) == 0 for w, d in DIL_GROUPS)
    cos, sin = _rope_tables(seq)
    row = lambda t: t.reshape(1, -1)
    bf = lambda t: t.astype(BF16)
    xt = x.reshape(batch * seq, D_MODEL)
    for l in range(depth):
        w_in_l = bf(w_in[l])
        xt, q_sb, k_sb, v_sb, q_dl, k_dl, v_dl = _ffn_proj(
            xt, row(g_ffn1[l]), bf(w1_a[l]), bf(w3_a[l]), bf(w2_a[l]),
            row(g_mix[l]), w_in_l[:, :QKV_W], cos, sin, seq)
        o_sb = _sb_attention(q_sb, k_sb, v_sb, batch, seq)
        o_dl = _dil_attention(q_dl, k_dl, v_dl, batch, seq)
        xt = _mix_ffn(xt, row(g_mix[l]), w_in_l[:, QKV_W:], row(b_gate[l]), o_sb, o_dl,
                      bf(w_sb_out[l]), bf(w_dil_out[l]), bf(w_o[l]),
                      row(g_ffn2[l]), bf(w1_b[l]), bf(w3_b[l]), bf(w2_b[l]),
                      row(g_final) if l == depth - 1 else None)
    return xt.reshape(batch, seq, D_MODEL)
```

```python
import functools

import jax
import jax.numpy as jnp
from jax import lax
from jax.experimental import pallas as pl
from jax.experimental.pallas import tpu as pltpu

D_MODEL = 1024
HEAD_DIM = 64
SB_HEADS = 8
DIL_GROUPS = ((128, 1), (512, 4), (2048, 16))
DIL_KV_HEADS = 4
D_FF = 2816
ROPE_THETA = 10000.0
RMS_EPS = 1e-6

SB_W = SB_HEADS * HEAD_DIM
DKV_W = DIL_KV_HEADS * HEAD_DIM
DQ_W = DKV_W * len(DIL_GROUPS)
QKV_W = 3 * SB_W + DQ_W + 2 * DKV_W
SCALE = HEAD_DIM ** -0.5

LANES = 128
TOKEN_TILE = 512
FF_CHUNK = 256
SB_BLOCK = 128
SB_GROUP = 8
DIL_BAND = 128
DIL_WINDOW = DIL_BAND * max(d for _, d in DIL_GROUPS)
DIL_BATCH = 4
DIL_SPLIT = 4
SB_LOG_FLOOR = -88.0
NEG_BIG = -1e30
VMEM_LIMIT = 56 * 1024 * 1024

F32 = jnp.float32
BF16 = jnp.bfloat16
_NT = (((1,), (1,)), ((), ()))


def _rmsnorm(x, g):
    ms = jnp.mean(x * x, axis=-1, keepdims=True)
    return x * lax.rsqrt(ms + RMS_EPS) * g


def _resident(shape):
    return pl.BlockSpec(shape, lambda *_: (0,) * len(shape), pipeline_mode=pl.Buffered(1))


def _resident_cols(rows, lo, width):
    return pl.BlockSpec((pl.Element(rows), pl.Element(width)), lambda *_: (0, lo),
                        pipeline_mode=pl.Buffered(1))


def _params(semantics):
    return pltpu.CompilerParams(dimension_semantics=semantics, vmem_limit_bytes=VMEM_LIMIT)


def _swiglu_half_step(x, g, w1_ref, w3_ref, w2_ref, act_ref):
    h = _rmsnorm(x, g).astype(BF16)
    for lo in range(0, D_FF, FF_CHUNK):
        sl = slice(lo, lo + FF_CHUNK)
        u = jnp.dot(h, w1_ref[:, sl], preferred_element_type=F32)
        v = jnp.dot(h, w3_ref[:, sl], preferred_element_type=F32)
        act_ref[:, sl] = (u * jax.nn.sigmoid(u) * v).astype(BF16)
    return x + 0.5 * jnp.dot(act_ref[...], w2_ref[...], preferred_element_type=F32)


def _ffn_weight_specs():
    return [_resident((D_MODEL, D_FF)), _resident((D_MODEL, D_FF)), _resident((D_FF, D_MODEL))]


def _token_tile(width):
    return pl.BlockSpec((TOKEN_TILE, width), lambda i: (i, 0))


def _rope_cols(t, cos, sin_signed):
    lane = lax.broadcasted_iota(jnp.int32, t.shape, 1)
    up = pltpu.roll(t, HEAD_DIM // 2, 1)
    down = pltpu.roll(t, LANES - HEAD_DIM // 2, 1)
    partner = jnp.where((lane & (HEAD_DIM // 2)) == 0, down, up)
    return t * cos + partner * sin_signed


def _ffn_proj_kernel(x_ref, g1_ref, w1_ref, w3_ref, w2_ref, gm_ref, w_ref, cos_ref, sin_ref,
                     x1_ref, qs_ref, ks_ref, vs_ref, qd_ref, kd_ref, vd_ref, act_ref):
    x1 = _swiglu_half_step(x_ref[...], g1_ref[...], w1_ref, w3_ref, w2_ref, act_ref)
    x1_ref[...] = x1
    h = _rmsnorm(x1, gm_ref[...]).astype(BF16)
    cos = cos_ref[...]
    sin = sin_ref[...]

    def cols(lo, width):
        return jnp.dot(h, w_ref[:, lo:lo + width], preferred_element_type=F32)

    base = 3 * SB_W
    qd = cols(base, DQ_W)
    kd = cols(base + DQ_W, DKV_W)
    for c in range(DQ_W // LANES):
        sl = slice(c * LANES, (c + 1) * LANES)
        qd_ref[:, sl] = _rope_cols(qd[:, sl], cos, sin) * SCALE
    for c in range(DKV_W // LANES):
        sl = slice(c * LANES, (c + 1) * LANES)
        kd_ref[:, sl] = _rope_cols(kd[:, sl], cos, sin)
    qs_ref[...] = (cols(0, SB_W) * SCALE).astype(BF16)
    ks_ref[...] = cols(SB_W, SB_W).astype(BF16)
    vs_ref[...] = cols(2 * SB_W, SB_W).astype(BF16)
    vd_ref[...] = cols(base + DQ_W + DKV_W, DKV_W)


def _ffn_proj(x, g1, w1, w3, w2, g_mix, w_in, cos, sin, seq):
    n = x.shape[0]
    pos_blocks = seq // TOKEN_TILE
    table = pl.BlockSpec((TOKEN_TILE, LANES), lambda i: (i % pos_blocks, 0))
    outs = ((D_MODEL, F32), (SB_W, BF16), (SB_W, BF16), (SB_W, BF16), (DQ_W, F32), (DKV_W, F32), (DKV_W, F32))
    return pl.pallas_call(
        _ffn_proj_kernel,
        out_shape=[jax.ShapeDtypeStruct((n, w), dt) for w, dt in outs],
        grid=(n // TOKEN_TILE,),
        in_specs=[_token_tile(D_MODEL), _resident((1, D_MODEL))] + _ffn_weight_specs()
                 + [_resident((1, D_MODEL)), _resident_cols(D_MODEL, 0, QKV_W), table, table],
        out_specs=[_token_tile(w) for w, _ in outs],
        scratch_shapes=[pltpu.VMEM((TOKEN_TILE, D_FF), BF16)],
        compiler_params=_params(("parallel",)),
        name="ffn_proj",
    )(x, g1, w1, w3, w2, g_mix, w_in, cos, sin)


def _split2(x):
    p0 = x.astype(BF16)
    p1 = (x - p0.astype(F32)).astype(BF16)
    return jnp.concatenate([p0, p1], axis=1)


def _log_terms(z):
    log_not = -(jnp.maximum(z, 0.0) + jnp.log(1.0 + jnp.exp(-jnp.abs(z))))
    return log_not, z + log_not


def _sb_kernel(q_ref, k_ref, v_ref, tri2_ref, tri_ref, o_ref, acc_ref, carry_ref):
    step = pl.program_id(2)
    lane = lax.broadcasted_iota(jnp.int32, (SB_BLOCK, LANES), 1)
    head_lanes = (lane < HEAD_DIM, lane >= HEAD_DIM)
    ahead = (lax.broadcasted_iota(jnp.int32, (SB_BLOCK, 2 * SB_BLOCK), 1)
             - lax.broadcasted_iota(jnp.int32, (SB_BLOCK, 2 * SB_BLOCK), 0))

    older, starts, stricts = [], [], []
    for g in range(SB_GROUP):
        blk = step * SB_GROUP + g
        first = jnp.maximum(blk - 1, 0)
        starts.append(pl.multiple_of(first * SB_BLOCK, SB_BLOCK))
        stricts.append(ahead < (blk - first) * SB_BLOCK)
        older.append(first - 1)
    pairs = [(g, hd) for g in range(SB_GROUP) for hd in range(2)]
    zs = []
    for g, hd in pairs:
        q = q_ref[g * SB_BLOCK:(g + 1) * SB_BLOCK, :]
        qh = jnp.where(head_lanes[hd], q, jnp.zeros_like(q))
        kwin = k_ref[pl.ds(starts[g], 2 * SB_BLOCK), :]
        zs.append(lax.dot_general(qh, kwin, _NT, preferred_element_type=F32))
    log_betas, pieces, tops = [], [], []
    for (g, hd), z in zip(pairs, zs):
        log_not, log_beta = _log_terms(jnp.where(stricts[g], z, NEG_BIG))
        log_betas.append(log_beta)
        pieces.append(_split2(log_not))
        total = jnp.sum(log_not, axis=1, keepdims=True)
        carry_ref[g, hd] = jnp.broadcast_to(total, (SB_BLOCK, LANES))
        tops.append(jnp.max(total))
    betweens = [jnp.dot(p, tri2_ref[...], preferred_element_type=F32) for p in pieces]
    weights = [jnp.exp(lb + bt).astype(BF16) for lb, bt in zip(log_betas, betweens)]
    for (g, hd), a in zip(pairs, weights):
        vwin = v_ref[pl.ds(starts[g], 2 * SB_BLOCK), :]
        acc_ref[g, hd] = jnp.dot(a, vwin, preferred_element_type=F32)
    go = [(jnp.maximum(tops[2 * g], tops[2 * g + 1]) > SB_LOG_FLOOR).astype(jnp.int32)
          for g in range(SB_GROUP)]

    for g in range(SB_GROUP):

        def visit(state, g=g):
            kb, _ = state
            q = q_ref[g * SB_BLOCK:(g + 1) * SB_BLOCK, :]
            at = pl.multiple_of(kb * SB_BLOCK, SB_BLOCK)
            kblk = k_ref[pl.ds(at, SB_BLOCK), :]
            vblk = v_ref[pl.ds(at, SB_BLOCK), :]
            worst = None
            for hd in range(2):
                qh = jnp.where(head_lanes[hd], q, jnp.zeros_like(q))
                z = lax.dot_general(qh, kblk, _NT, preferred_element_type=F32)
                log_not, log_beta = _log_terms(z)
                sums = jnp.dot(_split2(log_not), tri_ref[...], preferred_element_type=F32)
                carry = carry_ref[g, hd]
                a = jnp.exp(log_beta + sums[:, :SB_BLOCK] + carry)
                acc_ref[g, hd] += jnp.dot(a.astype(BF16), vblk, preferred_element_type=F32)
                carry = carry + sums[:, SB_BLOCK:]
                carry_ref[g, hd] = carry
                top = jnp.max(carry)
                worst = top if worst is None else jnp.maximum(worst, top)
            return kb - 1, (worst > SB_LOG_FLOOR).astype(jnp.int32)

        def live(state):
            kb, more = state
            return jnp.logical_and(kb >= 0, more > 0)

        lax.while_loop(live, visit, (older[g], go[g]))
        o_ref[g * SB_BLOCK:(g + 1) * SB_BLOCK, :] = jnp.where(
            head_lanes[0], acc_ref[g, 0], acc_ref[g, 1]).astype(o_ref.dtype)


def _suffix_matrix(n):
    j = lax.broadcasted_iota(jnp.int32, (n, n), 0)
    s = lax.broadcasted_iota(jnp.int32, (n, n), 1)
    return (j > s).astype(BF16)


def _sb_attention(q, k, v, batch, seq):
    q3, k3, v3 = (t.reshape(batch, seq, SB_W) for t in (q, k, v))
    tri2 = jnp.tile(_suffix_matrix(2 * SB_BLOCK), (2, 1))
    tri = jnp.tile(jnp.concatenate([_suffix_matrix(SB_BLOCK), jnp.ones((SB_BLOCK, LANES), BF16)], axis=1), (2, 1))
    rows = SB_GROUP * SB_BLOCK
    qblk = pl.BlockSpec((None, rows, LANES), lambda b, p, i: (b, i, p))
    whole = pl.BlockSpec((None, seq, LANES), lambda b, p, i: (b, 0, p))
    scratch = pltpu.VMEM((SB_GROUP, 2, SB_BLOCK, LANES), F32)
    out = pl.pallas_call(
        _sb_kernel,
        out_shape=jax.ShapeDtypeStruct((batch, seq, SB_W), BF16),
        grid=(batch, SB_W // LANES, seq // rows),
        in_specs=[qblk, whole, whole, _resident(tri2.shape), _resident(tri.shape)],
        out_specs=qblk,
        scratch_shapes=[scratch, scratch],
        compiler_params=_params(("parallel", "parallel", "arbitrary")),
        name="sb_attn",
    )(q3, k3, v3, tri2, tri)
    return out.reshape(batch * seq, SB_W)


def _strided(start, size, stride):
    return pl.ds(start, size) if stride == 1 else pl.ds(start, size, stride=stride)


def _band_attention(blocks, head_lanes):
    scores = []
    for q, kcat, _, _ in blocks:
        for hd in range(2):
            qh = jnp.where(head_lanes[hd], q, jnp.zeros_like(q))
            scores.append(lax.dot_general(qh, kcat, _NT, preferred_element_type=F32))
    probs, stats = [], []
    for i, s in enumerate(scores):
        s = jnp.where(blocks[i // 2][3], s, NEG_BIG)
        mx = jnp.max(s, axis=-1, keepdims=True)
        e = jnp.exp(s - mx)
        probs.append(e.astype(BF16))
        stats.append((mx, jnp.sum(e, axis=-1, keepdims=True)))
    results = []
    for b, (_, _, vcat, _) in enumerate(blocks):
        outs, lses = [], []
        for hd in range(2):
            mx, den = stats[2 * b + hd]
            o = jnp.dot(probs[2 * b + hd], vcat, preferred_element_type=F32)
            outs.append(o / den)
            lses.append(jnp.broadcast_to(mx + jnp.log(den), (DIL_BAND, LANES)))
        results.append((jnp.where(head_lanes[0], outs[0], outs[1]),
                        jnp.where(head_lanes[0], lses[0], lses[1])))
    return results


def _dil_kernel(q0_ref, q1_ref, q2_ref, kprev_ref, k_ref, vprev_ref, v_ref, o_ref, obuf, lbuf, staged):
    step = pl.program_id(2)
    row = lax.broadcasted_iota(jnp.int32, (DIL_BAND, 2 * DIL_BAND), 0)
    col = lax.broadcasted_iota(jnp.int32, (DIL_BAND, 2 * DIL_BAND), 1)
    band = jnp.logical_and(col >= row, col <= row + DIL_BAND)
    band_first = jnp.logical_and(band, jnp.logical_or(col >= DIL_BAND, step > 0))
    lane = lax.broadcasted_iota(jnp.int32, (DIL_BAND, LANES), 1)
    head_lanes = (lane < HEAD_DIM, lane >= HEAD_DIM)
    q_refs = (q0_ref, q1_ref, q2_ref)
    part = DIL_WINDOW // DIL_SPLIT

    def operands(g):
        return q_refs[g], kprev_ref, k_ref, vprev_ref, v_ref

    def fetch(g, dil, which, r, m0, n):
        if isinstance(r, tuple):
            lo, hi = r
            inner = dil // DIL_SPLIT
            return staged[which, _strided(lo * part + hi + inner * m0, n, inner), :]
        return operands(g)[which][_strided(r + dil * m0, n, dil), :]

    def load(g, dil, r, j):
        m0 = DIL_BAND * j
        if dil == 1 and not isinstance(j, int):
            m0 = pl.multiple_of(m0, DIL_BAND)
        token_r = r[0] + DIL_SPLIT * r[1] if isinstance(r, tuple) else r
        q = fetch(g, dil, 0, r, m0, DIL_BAND).astype(BF16)
        if isinstance(j, int) and j == 0:
            tail = DIL_WINDOW // dil - DIL_BAND
            kcat = jnp.concatenate([fetch(g, dil, 1, r, tail, DIL_BAND), fetch(g, dil, 2, r, 0, DIL_BAND)], axis=0)
            vcat = jnp.concatenate([fetch(g, dil, 3, r, tail, DIL_BAND), fetch(g, dil, 4, r, 0, DIL_BAND)], axis=0)
            valid = band_first
        else:
            kcat = fetch(g, dil, 2, r, m0 - DIL_BAND, 2 * DIL_BAND)
            vcat = fetch(g, dil, 4, r, m0 - DIL_BAND, 2 * DIL_BAND)
            valid = band
        return token_r + dil * m0, (q, kcat.astype(BF16), vcat.astype(BF16), valid)

    def run(g, dil, where):
        loaded = [load(g, dil, r, j) for r, j in where]
        results = _band_attention([blk for _, blk in loaded], head_lanes)
        for (base, _), (out, lse) in zip(loaded, results):
            obuf[g, _strided(base, DIL_BAND, dil), :] = out
            lbuf[g, _strided(base, DIL_BAND, dil), :] = lse

    for g, (_, dil) in enumerate(DIL_GROUPS):
        per_residue = DIL_WINDOW // (dil * DIL_BAND)
        if dil == 1:
            run(g, dil, [(0, j) for j in range(DIL_BATCH)])

            def body(i, _, g=g, dil=dil):
                run(g, dil, [(0, i * DIL_BATCH + j) for j in range(DIL_BATCH)])
                return 0

            lax.fori_loop(1, per_residue // DIL_BATCH, body, 0)
        elif dil <= DIL_SPLIT:
            def body(r, _, g=g, dil=dil, per_residue=per_residue):
                for j0 in range(0, per_residue, DIL_BATCH):
                    run(g, dil, [(r, j) for j in range(j0, min(j0 + DIL_BATCH, per_residue))])
                return 0

            lax.fori_loop(0, dil, body, 0)
        else:
            for which, ref in enumerate(operands(g)):
                for lo in range(DIL_SPLIT):
                    staged[which, lo * part:(lo + 1) * part, :] = ref[pl.ds(lo, part, stride=DIL_SPLIT), :]

            def body(hi, _, g=g, dil=dil, per_residue=per_residue):
                run(g, dil, [((lo, hi), j) for lo in range(DIL_SPLIT) for j in range(per_residue)])
                return 0

            lax.fori_loop(0, dil // DIL_SPLIT, body, 0)

    chunk = 2 * DIL_BAND
    for c in range(DIL_WINDOW // chunk):
        rows = slice(c * chunk, (c + 1) * chunk)
        lses = [lbuf[g, rows, :] for g in range(len(DIL_GROUPS))]
        top = functools.reduce(jnp.maximum, lses)
        ws = [jnp.exp(l - top) for l in lses]
        num = sum(w * obuf[g, rows, :] for g, w in enumerate(ws))
        o_ref[rows, :] = (num / sum(ws)).astype(o_ref.dtype)


def _dil_attention(q, k, v, batch, seq):
    n_groups = len(DIL_GROUPS)
    pairs = DKV_W // LANES
    q3 = q.reshape(batch, seq, DQ_W)
    k3 = k.reshape(batch, seq, DKV_W)
    v3 = v.reshape(batch, seq, DKV_W)

    def q_spec(g):
        return pl.BlockSpec((None, DIL_WINDOW, LANES), lambda b, p, i: (b, i, g * pairs + p))

    cur = pl.BlockSpec((None, DIL_WINDOW, LANES), lambda b, p, i: (b, i, p))
    prev = pl.BlockSpec((None, DIL_WINDOW, LANES), lambda b, p, i: (b, jnp.maximum(i - 1, 0), p))
    buf = pltpu.VMEM((n_groups, DIL_WINDOW, LANES), F32)
    staged = pltpu.VMEM((5, DIL_WINDOW, LANES), F32)
    out = pl.pallas_call(
        _dil_kernel,
        out_shape=jax.ShapeDtypeStruct((batch, seq, DKV_W), BF16),
        grid=(batch, pairs, seq // DIL_WINDOW),
        in_specs=[q_spec(g) for g in range(n_groups)] + [prev, cur, prev, cur],
        out_specs=cur,
        scratch_shapes=[buf, buf, staged],
        compiler_params=_params(("parallel", "parallel", "arbitrary")),
        name="dil_attn",
    )(q3, q3, q3, k3, k3, v3, v3)
    return out.reshape(batch * seq, DKV_W)


def _mix_ffn_kernel(*refs, final_norm):
    (x_ref, gm_ref, wg_ref, bg_ref, osb_ref, odl_ref, wsb_ref, wdl_ref, wo_ref,
     g2_ref, w1_ref, w3_ref, w2_ref) = refs[:13]
    gf_ref = refs[13] if final_norm else None
    o_ref, act_ref = refs[-2:]
    x = x_ref[...]
    h = _rmsnorm(x, gm_ref[...]).astype(BF16)

    def gate(lo):
        pre = jnp.dot(h, wg_ref[:, lo:lo + D_MODEL], preferred_element_type=F32)
        return jax.nn.sigmoid(pre + bg_ref[:, lo:lo + D_MODEL])

    mixed = gate(0) * jnp.dot(osb_ref[...], wsb_ref[...], preferred_element_type=F32)
    mixed = mixed + gate(D_MODEL) * jnp.dot(odl_ref[...], wdl_ref[...], preferred_element_type=F32)
    x2 = x + jnp.dot(mixed.astype(BF16), wo_ref[...], preferred_element_type=F32)
    y = _swiglu_half_step(x2, g2_ref[...], w1_ref, w3_ref, w2_ref, act_ref)
    if final_norm:
        y = _rmsnorm(y, gf_ref[...])
    o_ref[...] = y


def _mix_ffn(x, g_mix, w_in, b_gate, o_sb, o_dl, w_sb_out, w_dil_out, w_o, g2, w1, w3, w2, g_final=None):
    n = x.shape[0]
    in_specs = ([_token_tile(D_MODEL), _resident((1, D_MODEL)), _resident_cols(D_MODEL, QKV_W, 2 * D_MODEL),
                 _resident((1, 2 * D_MODEL)), _token_tile(SB_W), _token_tile(DKV_W),
                 _resident((SB_W, D_MODEL)), _resident((DKV_W, D_MODEL)), _resident((D_MODEL, D_MODEL)),
                 _resident((1, D_MODEL))] + _ffn_weight_specs())
    args = [x, g_mix, w_in, b_gate, o_sb, o_dl, w_sb_out, w_dil_out, w_o, g2, w1, w3, w2]
    if g_final is not None:
        in_specs.append(_resident((1, D_MODEL)))
        args.append(g_final)
    return pl.pallas_call(
        functools.partial(_mix_ffn_kernel, final_norm=g_final is not None),
        out_shape=jax.ShapeDtypeStruct((n, D_MODEL), F32),
        grid=(n // TOKEN_TILE,),
        in_specs=in_specs,
        out_specs=_token_tile(D_MODEL),
        scratch_shapes=[pltpu.VMEM((TOKEN_TILE, D_FF), BF16)],
        compiler_params=_params(("parallel",)),
        name="mix_ffn",
    )(*args)


def _rope_tables(seq):
    half = HEAD_DIM // 2
    inv_freq = ROPE_THETA ** (-jnp.arange(half, dtype=F32) / half)
    lane_freq = jnp.tile(inv_freq, LANES // half)
    sign = jnp.tile(jnp.concatenate([-jnp.ones((half,), F32), jnp.ones((half,), F32)]), LANES // HEAD_DIM)
    ang = jnp.arange(seq, dtype=F32)[:, None] * lane_freq[None, :]
    return jnp.cos(ang), jnp.sin(ang) * sign[None, :]


def kernel(x, g_ffn1, w1_a, w3_a, w2_a, g_mix, w_in, b_gate, w_sb_out, w_dil_out, w_o,
           g_ffn2, w1_b, w3_b, w2_b, g_final):
    batch, seq, _ = x.shape
    depth = g_ffn1.shape[0]
    assert seq % DIL_WINDOW == 0 and seq % (SB_GROUP * SB_BLOCK) == 0 and seq % TOKEN_TILE == 0
    assert all(w == DIL_BAND * d and DIL_WINDOW % (DIL_BAND * d) == 0 for w, d in DIL_GROUPS)
    assert all(d <= DIL_SPLIT or (d % DIL_SPLIT == 0 and d // DIL_SPLIT <= DIL_SPLIT) for _, d in DIL_GROUPS)
    cos, sin = _rope_tables(seq)
    row = lambda t: t.reshape(1, -1)
    bf = lambda t: t.astype(BF16)
    xt = x.reshape(batch * seq, D_MODEL)
    for l in range(depth):
        w_in_l = bf(w_in[l])
        xt, q_sb, k_sb, v_sb, q_dl, k_dl, v_dl = _ffn_proj(
            xt, row(g_ffn1[l]), bf(w1_a[l]), bf(w3_a[l]), bf(w2_a[l]),
            row(g_mix[l]), w_in_l, cos, sin, seq)
        o_sb = _sb_attention(q_sb, k_sb, v_sb, batch, seq)
        o_dl = _dil_attention(q_dl, k_dl, v_dl, batch, seq)
        xt = _mix_ffn(xt, row(g_mix[l]), w_in_l, row(b_gate[l]), o_sb, o_dl,
                      bf(w_sb_out[l]), bf(w_dil_out[l]), bf(w_o[l]),
                      row(g_ffn2[l]), bf(w1_b[l]), bf(w3_b[l]), bf(w2_b[l]),
                      row(g_final) if l == depth - 1 else None)
    return xt.reshape(batch, seq, D_MODEL)
```

```python
import functools

import jax
import jax.numpy as jnp
from jax import lax
from jax.experimental import pallas as pl
from jax.experimental.pallas import tpu as pltpu

D_MODEL = 1024
HEAD_DIM = 64
SB_HEADS = 8
DIL_GROUPS = ((128, 1), (512, 4), (2048, 16))
DIL_KV_HEADS = 4
D_FF = 2816
ROPE_THETA = 10000.0
RMS_EPS = 1e-6

SB_W = SB_HEADS * HEAD_DIM
DKV_W = DIL_KV_HEADS * HEAD_DIM
DQ_W = DKV_W * len(DIL_GROUPS)
QKV_W = 3 * SB_W + DQ_W + 2 * DKV_W
SCALE = HEAD_DIM ** -0.5

LANES = 128
TOKEN_TILE = 512
FF_CHUNK = 256
SB_BLOCK = 128
SB_GROUP = 16
DIL_BAND = 128
DIL_WINDOW = DIL_BAND * max(d for _, d in DIL_GROUPS)
DIL_BATCH = 4
DIL_SPLIT = 4
SB_LOG_FLOOR = -88.0
NEG_BIG = -1e30
VMEM_LIMIT = 56 * 1024 * 1024

F32 = jnp.float32
BF16 = jnp.bfloat16
_NT = (((1,), (1,)), ((), ()))


def _rmsnorm(x, g):
    ms = jnp.mean(x * x, axis=-1, keepdims=True)
    return x * lax.rsqrt(ms + RMS_EPS) * g


def _resident(shape):
    return pl.BlockSpec(shape, lambda *_: (0,) * len(shape), pipeline_mode=pl.Buffered(1))


def _resident_cols(rows, lo, width):
    return pl.BlockSpec((pl.Element(rows), pl.Element(width)), lambda *_: (0, lo),
                        pipeline_mode=pl.Buffered(1))


def _params(semantics):
    return pltpu.CompilerParams(dimension_semantics=semantics, vmem_limit_bytes=VMEM_LIMIT)


def _swiglu_half_step(x, g, w1_ref, w3_ref, w2_ref, act_ref):
    h = _rmsnorm(x, g).astype(BF16)
    for lo in range(0, D_FF, FF_CHUNK):
        sl = slice(lo, lo + FF_CHUNK)
        u = jnp.dot(h, w1_ref[:, sl], preferred_element_type=F32)
        v = jnp.dot(h, w3_ref[:, sl], preferred_element_type=F32)
        act_ref[:, sl] = (u * jax.nn.sigmoid(u) * v).astype(BF16)
    return x + 0.5 * jnp.dot(act_ref[...], w2_ref[...], preferred_element_type=F32)


def _ffn_weight_specs():
    return [_resident((D_MODEL, D_FF)), _resident((D_MODEL, D_FF)), _resident((D_FF, D_MODEL))]


def _token_tile(width):
    return pl.BlockSpec((TOKEN_TILE, width), lambda i: (i, 0))


def _rope_cols(t, cos, sin_signed):
    lane = lax.broadcasted_iota(jnp.int32, t.shape, 1)
    up = pltpu.roll(t, HEAD_DIM // 2, 1)
    down = pltpu.roll(t, LANES - HEAD_DIM // 2, 1)
    partner = jnp.where((lane & (HEAD_DIM // 2)) == 0, down, up)
    return t * cos + partner * sin_signed


def _ffn_proj_kernel(x_ref, g1_ref, w1_ref, w3_ref, w2_ref, gm_ref, w_ref, cos_ref, sin_ref,
                     x1_ref, qs_ref, ks_ref, vs_ref, qd_ref, kd_ref, vd_ref, act_ref):
    x1 = _swiglu_half_step(x_ref[...], g1_ref[...], w1_ref, w3_ref, w2_ref, act_ref)
    x1_ref[...] = x1
    h = _rmsnorm(x1, gm_ref[...]).astype(BF16)
    cos = jnp.concatenate([cos_ref[...]] * (LANES // cos_ref.shape[1]), axis=1)
    sin = jnp.concatenate([-sin_ref[...], sin_ref[...]] * (LANES // HEAD_DIM), axis=1)

    def cols(lo, width):
        return jnp.dot(h, w_ref[:, lo:lo + width], preferred_element_type=F32)

    base = 3 * SB_W
    qd = cols(base, DQ_W)
    kd = cols(base + DQ_W, DKV_W)
    for c in range(DQ_W // LANES):
        sl = slice(c * LANES, (c + 1) * LANES)
        qd_ref[:, sl] = _rope_cols(qd[:, sl], cos, sin) * SCALE
    for c in range(DKV_W // LANES):
        sl = slice(c * LANES, (c + 1) * LANES)
        kd_ref[:, sl] = _rope_cols(kd[:, sl], cos, sin)
    qs_ref[...] = (cols(0, SB_W) * SCALE).astype(BF16)
    ks_ref[...] = cols(SB_W, SB_W).astype(BF16)
    vs_ref[...] = cols(2 * SB_W, SB_W).astype(BF16)
    vd_ref[...] = cols(base + DQ_W + DKV_W, DKV_W)


def _ffn_proj(x, g1, w1, w3, w2, g_mix, w_in, cos, sin, seq):
    n = x.shape[0]
    pos_blocks = seq // TOKEN_TILE
    table = pl.BlockSpec((TOKEN_TILE, HEAD_DIM // 2), lambda i: (i % pos_blocks, 0))
    outs = ((D_MODEL, F32), (SB_W, BF16), (SB_W, BF16), (SB_W, BF16), (DQ_W, F32), (DKV_W, F32), (DKV_W, F32))
    return pl.pallas_call(
        _ffn_proj_kernel,
        out_shape=[jax.ShapeDtypeStruct((n, w), dt) for w, dt in outs],
        grid=(n // TOKEN_TILE,),
        in_specs=[_token_tile(D_MODEL), _resident((1, D_MODEL))] + _ffn_weight_specs()
                 + [_resident((1, D_MODEL)), _resident_cols(D_MODEL, 0, QKV_W), table, table],
        out_specs=[_token_tile(w) for w, _ in outs],
        scratch_shapes=[pltpu.VMEM((TOKEN_TILE, D_FF), BF16)],
        compiler_params=_params(("parallel",)),
        name="ffn_proj",
    )(x, g1, w1, w3, w2, g_mix, w_in, cos, sin)


def _split2(x):
    p0 = x.astype(BF16)
    p1 = (x - p0.astype(F32)).astype(BF16)
    return jnp.concatenate([p0, p1], axis=1)


def _log_terms(z):
    log_not = -(jnp.maximum(z, 0.0) + jnp.log(1.0 + jnp.exp(-jnp.abs(z))))
    return log_not, z + log_not


def _sb_kernel(q_ref, k_ref, v_ref, tri2_ref, tri_ref, o_ref, acc_ref, carry_ref):
    step = pl.program_id(2)
    lane = lax.broadcasted_iota(jnp.int32, (SB_BLOCK, LANES), 1)
    head_lanes = (lane < HEAD_DIM, lane >= HEAD_DIM)
    ahead = (lax.broadcasted_iota(jnp.int32, (SB_BLOCK, 2 * SB_BLOCK), 1)
             - lax.broadcasted_iota(jnp.int32, (SB_BLOCK, 2 * SB_BLOCK), 0))

    older, starts, stricts = [], [], []
    for g in range(SB_GROUP):
        blk = step * SB_GROUP + g
        first = jnp.maximum(blk - 1, 0)
        starts.append(pl.multiple_of(first * SB_BLOCK, SB_BLOCK))
        stricts.append(ahead < (blk - first) * SB_BLOCK)
        older.append(first - 1)
    pairs = [(g, hd) for g in range(SB_GROUP) for hd in range(2)]
    zs = []
    for g, hd in pairs:
        q = q_ref[g * SB_BLOCK:(g + 1) * SB_BLOCK, :]
        qh = jnp.where(head_lanes[hd], q, jnp.zeros_like(q))
        kwin = k_ref[pl.ds(starts[g], 2 * SB_BLOCK), :]
        zs.append(lax.dot_general(qh, kwin, _NT, preferred_element_type=F32))
    log_betas, pieces, tops = [], [], []
    for (g, hd), z in zip(pairs, zs):
        log_not, log_beta = _log_terms(jnp.where(stricts[g], z, NEG_BIG))
        log_betas.append(log_beta)
        pieces.append(_split2(log_not))
        total = jnp.sum(log_not, axis=1, keepdims=True)
        carry_ref[g, hd] = jnp.broadcast_to(total, (SB_BLOCK, LANES))
        tops.append(jnp.max(total))
    betweens = [jnp.dot(p, tri2_ref[...], preferred_element_type=F32) for p in pieces]
    weights = [jnp.exp(lb + bt).astype(BF16) for lb, bt in zip(log_betas, betweens)]
    for (g, hd), a in zip(pairs, weights):
        vwin = v_ref[pl.ds(starts[g], 2 * SB_BLOCK), :]
        acc_ref[g, hd] = jnp.dot(a, vwin, preferred_element_type=F32)
    go = [(jnp.maximum(tops[2 * g], tops[2 * g + 1]) > SB_LOG_FLOOR).astype(jnp.int32)
          for g in range(SB_GROUP)]

    for g in range(SB_GROUP):

        def visit(state, g=g):
            kb, _ = state
            q = q_ref[g * SB_BLOCK:(g + 1) * SB_BLOCK, :]
            at = pl.multiple_of(kb * SB_BLOCK, SB_BLOCK)
            kblk = k_ref[pl.ds(at, SB_BLOCK), :]
            vblk = v_ref[pl.ds(at, SB_BLOCK), :]
            worst = None
            for hd in range(2):
                qh = jnp.where(head_lanes[hd], q, jnp.zeros_like(q))
                z = lax.dot_general(qh, kblk, _NT, preferred_element_type=F32)
                log_not, log_beta = _log_terms(z)
                sums = jnp.dot(_split2(log_not), tri_ref[...], preferred_element_type=F32)
                carry = carry_ref[g, hd]
                a = jnp.exp(log_beta + sums[:, :SB_BLOCK] + carry)
                acc_ref[g, hd] += jnp.dot(a.astype(BF16), vblk, preferred_element_type=F32)
                carry = carry + sums[:, SB_BLOCK:]
                carry_ref[g, hd] = carry
                top = jnp.max(carry)
                worst = top if worst is None else jnp.maximum(worst, top)
            return kb - 1, (worst > SB_LOG_FLOOR).astype(jnp.int32)

        def live(state):
            kb, more = state
            return jnp.logical_and(kb >= 0, more > 0)

        lax.while_loop(live, visit, (older[g], go[g]))
        o_ref[g * SB_BLOCK:(g + 1) * SB_BLOCK, :] = jnp.where(
            head_lanes[0], acc_ref[g, 0], acc_ref[g, 1]).astype(o_ref.dtype)


def _suffix_matrix(n):
    j = lax.broadcasted_iota(jnp.int32, (n, n), 0)
    s = lax.broadcasted_iota(jnp.int32, (n, n), 1)
    return (j > s).astype(BF16)


def _sb_attention(q, k, v, batch, seq):
    q3, k3, v3 = (t.reshape(batch, seq, SB_W) for t in (q, k, v))
    tri2 = jnp.tile(_suffix_matrix(2 * SB_BLOCK), (2, 1))
    tri = jnp.tile(jnp.concatenate([_suffix_matrix(SB_BLOCK), jnp.ones((SB_BLOCK, LANES), BF16)], axis=1), (2, 1))
    rows = SB_GROUP * SB_BLOCK
    qblk = pl.BlockSpec((None, rows, LANES), lambda b, p, i: (b, i, p))
    whole = pl.BlockSpec((None, seq, LANES), lambda b, p, i: (b, 0, p))
    scratch = pltpu.VMEM((SB_GROUP, 2, SB_BLOCK, LANES), F32)
    out = pl.pallas_call(
        _sb_kernel,
        out_shape=jax.ShapeDtypeStruct((batch, seq, SB_W), BF16),
        grid=(batch, SB_W // LANES, seq // rows),
        in_specs=[qblk, whole, whole, _resident(tri2.shape), _resident(tri.shape)],
        out_specs=qblk,
        scratch_shapes=[scratch, scratch],
        compiler_params=_params(("parallel", "parallel", "arbitrary")),
        name="sb_attn",
    )(q3, k3, v3, tri2, tri)
    return out.reshape(batch * seq, SB_W)


def _strided(start, size, stride):
    return pl.ds(start, size) if stride == 1 else pl.ds(start, size, stride=stride)


def _band_attention(blocks, head_lanes):
    scores = []
    for q, kcat, _, _ in blocks:
        for hd in range(2):
            qh = jnp.where(head_lanes[hd], q, jnp.zeros_like(q))
            scores.append(lax.dot_general(qh, kcat, _NT, preferred_element_type=F32))
    probs, stats = [], []
    for i, s in enumerate(scores):
        s = jnp.where(blocks[i // 2][3], s, NEG_BIG)
        mx = jnp.max(s, axis=-1, keepdims=True)
        e = jnp.exp(s - mx)
        probs.append(e.astype(BF16))
        stats.append((mx, jnp.sum(e, axis=-1, keepdims=True)))
    results = []
    for b, (_, _, vcat, _) in enumerate(blocks):
        outs, lses = [], []
        for hd in range(2):
            mx, den = stats[2 * b + hd]
            o = jnp.dot(probs[2 * b + hd], vcat, preferred_element_type=F32)
            outs.append(o / den)
            lses.append(jnp.broadcast_to(mx + jnp.log(den), (DIL_BAND, LANES)))
        results.append((jnp.where(head_lanes[0], outs[0], outs[1]),
                        jnp.where(head_lanes[0], lses[0], lses[1])))
    return results


def _dil_kernel(q0_ref, q1_ref, q2_ref, kprev_ref, k_ref, vprev_ref, v_ref, o_ref, obuf, lbuf, staged):
    step = pl.program_id(2)
    row = lax.broadcasted_iota(jnp.int32, (DIL_BAND, 2 * DIL_BAND), 0)
    col = lax.broadcasted_iota(jnp.int32, (DIL_BAND, 2 * DIL_BAND), 1)
    band = jnp.logical_and(col >= row, col <= row + DIL_BAND)
    band_first = jnp.logical_and(band, jnp.logical_or(col >= DIL_BAND, step > 0))
    lane = lax.broadcasted_iota(jnp.int32, (DIL_BAND, LANES), 1)
    head_lanes = (lane < HEAD_DIM, lane >= HEAD_DIM)
    q_refs = (q0_ref, q1_ref, q2_ref)
    part = DIL_WINDOW // DIL_SPLIT

    def operands(g):
        return q_refs[g], kprev_ref, k_ref, vprev_ref, v_ref

    def fetch(g, dil, which, r, m0, n):
        if isinstance(r, tuple):
            lo, hi = r
            inner = dil // DIL_SPLIT
            return staged[which, _strided(lo * part + hi + inner * m0, n, inner), :]
        return operands(g)[which][_strided(r + dil * m0, n, dil), :]

    def load(g, dil, r, j):
        m0 = DIL_BAND * j
        if dil == 1 and not isinstance(j, int):
            m0 = pl.multiple_of(m0, DIL_BAND)
        token_r = r[0] + DIL_SPLIT * r[1] if isinstance(r, tuple) else r
        q = fetch(g, dil, 0, r, m0, DIL_BAND).astype(BF16)
        if isinstance(j, int) and j == 0:
            tail = DIL_WINDOW // dil - DIL_BAND
            kcat = jnp.concatenate([fetch(g, dil, 1, r, tail, DIL_BAND), fetch(g, dil, 2, r, 0, DIL_BAND)], axis=0)
            vcat = jnp.concatenate([fetch(g, dil, 3, r, tail, DIL_BAND), fetch(g, dil, 4, r, 0, DIL_BAND)], axis=0)
            valid = band_first
        else:
            kcat = fetch(g, dil, 2, r, m0 - DIL_BAND, 2 * DIL_BAND)
            vcat = fetch(g, dil, 4, r, m0 - DIL_BAND, 2 * DIL_BAND)
            valid = band
        return token_r + dil * m0, (q, kcat.astype(BF16), vcat.astype(BF16), valid)

    def run(g, dil, where):
        loaded = [load(g, dil, r, j) for r, j in where]
        results = _band_attention([blk for _, blk in loaded], head_lanes)
        for (base, _), (out, lse) in zip(loaded, results):
            obuf[g, _strided(base, DIL_BAND, dil), :] = out
            lbuf[g, _strided(base, DIL_BAND, dil), :] = lse

    for g, (_, dil) in enumerate(DIL_GROUPS):
        per_residue = DIL_WINDOW // (dil * DIL_BAND)
        if dil == 1:
            run(g, dil, [(0, j) for j in range(DIL_BATCH)])

            def body(i, _, g=g, dil=dil):
                run(g, dil, [(0, i * DIL_BATCH + j) for j in range(DIL_BATCH)])
                return 0

            lax.fori_loop(1, per_residue // DIL_BATCH, body, 0)
        elif dil <= DIL_SPLIT:
            def body(r, _, g=g, dil=dil, per_residue=per_residue):
                for j0 in range(0, per_residue, DIL_BATCH):
                    run(g, dil, [(r, j) for j in range(j0, min(j0 + DIL_BATCH, per_residue))])
                return 0

            lax.fori_loop(0, dil, body, 0)
        else:
            for which, ref in enumerate(operands(g)):
                for lo in range(DIL_SPLIT):
                    staged[which, lo * part:(lo + 1) * part, :] = ref[pl.ds(lo, part, stride=DIL_SPLIT), :]

            def body(hi, _, g=g, dil=dil, per_residue=per_residue):
                run(g, dil, [((lo, hi), j) for lo in range(DIL_SPLIT) for j in range(per_residue)])
                return 0

            lax.fori_loop(0, dil // DIL_SPLIT, body, 0)

    chunk = 2 * DIL_BAND
    for c in range(DIL_WINDOW // chunk):
        rows = slice(c * chunk, (c + 1) * chunk)
        lses = [lbuf[g, rows, :] for g in range(len(DIL_GROUPS))]
        top = functools.reduce(jnp.maximum, lses)
        ws = [jnp.exp(l - top) for l in lses]
        num = sum(w * obuf[g, rows, :] for g, w in enumerate(ws))
        o_ref[rows, :] = (num / sum(ws)).astype(o_ref.dtype)


def _dil_attention(q, k, v, batch, seq):
    n_groups = len(DIL_GROUPS)
    pairs = DKV_W // LANES
    q3 = q.reshape(batch, seq, DQ_W)
    k3 = k.reshape(batch, seq, DKV_W)
    v3 = v.reshape(batch, seq, DKV_W)

    def q_spec(g):
        return pl.BlockSpec((None, DIL_WINDOW, LANES), lambda b, p, i: (b, i, g * pairs + p))

    cur = pl.BlockSpec((None, DIL_WINDOW, LANES), lambda b, p, i: (b, i, p))
    prev = pl.BlockSpec((None, DIL_WINDOW, LANES), lambda b, p, i: (b, jnp.maximum(i - 1, 0), p))
    buf = pltpu.VMEM((n_groups, DIL_WINDOW, LANES), F32)
    staged = pltpu.VMEM((5, DIL_WINDOW, LANES), F32)
    out = pl.pallas_call(
        _dil_kernel,
        out_shape=jax.ShapeDtypeStruct((batch, seq, DKV_W), BF16),
        grid=(batch, pairs, seq // DIL_WINDOW),
        in_specs=[q_spec(g) for g in range(n_groups)] + [prev, cur, prev, cur],
        out_specs=cur,
        scratch_shapes=[buf, buf, staged],
        compiler_params=_params(("parallel", "parallel", "arbitrary")),
        name="dil_attn",
    )(q3, q3, q3, k3, k3, v3, v3)
    return out.reshape(batch * seq, DKV_W)


def _mix_ffn_kernel(*refs, final_norm):
    (x_ref, gm_ref, wg_ref, bg_ref, osb_ref, odl_ref, wsb_ref, wdl_ref, wo_ref,
     g2_ref, w1_ref, w3_ref, w2_ref) = refs[:13]
    gf_ref = refs[13] if final_norm else None
    o_ref, act_ref = refs[-2:]
    x = x_ref[...]
    h = _rmsnorm(x, gm_ref[...]).astype(BF16)

    def gate(lo):
        pre = jnp.dot(h, wg_ref[:, lo:lo + D_MODEL], preferred_element_type=F32)
        return jax.nn.sigmoid(pre + bg_ref[:, lo:lo + D_MODEL])

    mixed = gate(0) * jnp.dot(osb_ref[...], wsb_ref[...], preferred_element_type=F32)
    mixed = mixed + gate(D_MODEL) * jnp.dot(odl_ref[...], wdl_ref[...], preferred_element_type=F32)
    x2 = x + jnp.dot(mixed.astype(BF16), wo_ref[...], preferred_element_type=F32)
    y = _swiglu_half_step(x2, g2_ref[...], w1_ref, w3_ref, w2_ref, act_ref)
    if final_norm:
        y = _rmsnorm(y, gf_ref[...])
    o_ref[...] = y


def _mix_ffn(x, g_mix, w_in, b_gate, o_sb, o_dl, w_sb_out, w_dil_out, w_o, g2, w1, w3, w2, g_final=None):
    n = x.shape[0]
    in_specs = ([_token_tile(D_MODEL), _resident((1, D_MODEL)), _resident_cols(D_MODEL, QKV_W, 2 * D_MODEL),
                 _resident((1, 2 * D_MODEL)), _token_tile(SB_W), _token_tile(DKV_W),
                 _resident((SB_W, D_MODEL)), _resident((DKV_W, D_MODEL)), _resident((D_MODEL, D_MODEL)),
                 _resident((1, D_MODEL))] + _ffn_weight_specs())
    args = [x, g_mix, w_in, b_gate, o_sb, o_dl, w_sb_out, w_dil_out, w_o, g2, w1, w3, w2]
    if g_final is not None:
        in_specs.append(_resident((1, D_MODEL)))
        args.append(g_final)
    return pl.pallas_call(
        functools.partial(_mix_ffn_kernel, final_norm=g_final is not None),
        out_shape=jax.ShapeDtypeStruct((n, D_MODEL), F32),
        grid=(n // TOKEN_TILE,),
        in_specs=in_specs,
        out_specs=_token_tile(D_MODEL),
        scratch_shapes=[pltpu.VMEM((TOKEN_TILE, D_FF), BF16)],
        compiler_params=_params(("parallel",)),
        name="mix_ffn",
    )(*args)


def _rope_tables(seq):
    half = HEAD_DIM // 2
    inv_freq = ROPE_THETA ** (-jnp.arange(half, dtype=F32) / half)
    ang = jnp.arange(seq, dtype=F32)[:, None] * inv_freq[None, :]
    return jnp.cos(ang), jnp.sin(ang)


def kernel(x, g_ffn1, w1_a, w3_a, w2_a, g_mix, w_in, b_gate, w_sb_out, w_dil_out, w_o,
           g_ffn2, w1_b, w3_b, w2_b, g_final):
    batch, seq, _ = x.shape
    depth = g_ffn1.shape[0]
    assert seq % DIL_WINDOW == 0 and seq % (SB_GROUP * SB_BLOCK) == 0 and seq % TOKEN_TILE == 0
    assert all(w == DIL_BAND * d and DIL_WINDOW % (DIL_BAND * d) == 0 for w, d in DIL_GROUPS)
    assert all(d <= DIL_SPLIT or (d % DIL_SPLIT == 0 and d // DIL_SPLIT <= DIL_SPLIT) for _, d in DIL_GROUPS)
    cos, sin = _rope_tables(seq)
    row = lambda t: t.reshape(1, -1)
    bf = lambda t: t.astype(BF16)
    xt = x.reshape(batch * seq, D_MODEL)
    for l in range(depth):
        w_in_l = bf(w_in[l])
        xt, q_sb, k_sb, v_sb, q_dl, k_dl, v_dl = _ffn_proj(
            xt, row(g_ffn1[l]), bf(w1_a[l]), bf(w3_a[l]), bf(w2_a[l]),
            row(g_mix[l]), w_in_l, cos, sin, seq)
        o_sb = _sb_attention(q_sb, k_sb, v_sb, batch, seq)
        o_dl = _dil_attention(q_dl, k_dl, v_dl, batch, seq)
        xt = _mix_ffn(xt, row(g_mix[l]), w_in_l, row(b_gate[l]), o_sb, o_dl,
                      bf(w_sb_out[l]), bf(w_dil_out[l]), bf(w_o[l]),
                      row(g_ffn2[l]), bf(w1_b[l]), bf(w3_b[l]), bf(w2_b[l]),
                      row(g_final) if l == depth - 1 else None)
    return xt.reshape(batch, seq, D_MODEL)
```

```python
import functools

import jax
import jax.numpy as jnp
from jax import lax
from jax.experimental import pallas as pl
from jax.experimental.pallas import tpu as pltpu

D_MODEL = 1024
HEAD_DIM = 64
SB_HEADS = 8
DIL_GROUPS = ((128, 1), (512, 4), (2048, 16))
DIL_KV_HEADS = 4
D_FF = 2816
ROPE_THETA = 10000.0
RMS_EPS = 1e-6

SB_W = SB_HEADS * HEAD_DIM
DKV_W = DIL_KV_HEADS * HEAD_DIM
DQ_W = DKV_W * len(DIL_GROUPS)
QKV_W = 3 * SB_W + DQ_W + 2 * DKV_W
SCALE = HEAD_DIM ** -0.5

LANES = 128
TOKEN_TILE = 512
FF_CHUNK = 256
SB_BLOCK = 128
SB_GROUP = 16
DIL_BAND = 128
DIL_WINDOW = DIL_BAND * max(d for _, d in DIL_GROUPS)
DIL_BATCH = 4
DIL_SPLIT = 4
SB_LOG_FLOOR = -88.0
NEG_BIG = -1e30
VMEM_LIMIT = 56 * 1024 * 1024

F32 = jnp.float32
BF16 = jnp.bfloat16
_NT = (((1,), (1,)), ((), ()))


def _rmsnorm(x, g):
    ms = jnp.mean(x * x, axis=-1, keepdims=True)
    return x * lax.rsqrt(ms + RMS_EPS) * g


def _resident(shape):
    return pl.BlockSpec(shape, lambda *_: (0,) * len(shape), pipeline_mode=pl.Buffered(1))


def _resident_cols(rows, lo, width):
    return pl.BlockSpec((pl.Element(rows), pl.Element(width)), lambda *_: (0, lo),
                        pipeline_mode=pl.Buffered(1))


def _params(semantics):
    return pltpu.CompilerParams(dimension_semantics=semantics, vmem_limit_bytes=VMEM_LIMIT)


def _swiglu_half_step(x, g, w1_ref, w3_ref, w2_ref, act_ref):
    h = _rmsnorm(x, g).astype(BF16)
    for lo in range(0, D_FF, FF_CHUNK):
        sl = slice(lo, lo + FF_CHUNK)
        u = jnp.dot(h, w1_ref[:, sl], preferred_element_type=F32)
        v = jnp.dot(h, w3_ref[:, sl], preferred_element_type=F32)
        act_ref[:, sl] = (u * jax.nn.sigmoid(u) * v).astype(BF16)
    return x + 0.5 * jnp.dot(act_ref[...], w2_ref[...], preferred_element_type=F32)


def _ffn_weight_specs():
    return [_resident((D_MODEL, D_FF)), _resident((D_MODEL, D_FF)), _resident((D_FF, D_MODEL))]


def _token_tile(width):
    return pl.BlockSpec((TOKEN_TILE, width), lambda i: (i, 0))


def _rope_cols(t, cos, sin_signed):
    lane = lax.broadcasted_iota(jnp.int32, t.shape, 1)
    up = pltpu.roll(t, HEAD_DIM // 2, 1)
    down = pltpu.roll(t, LANES - HEAD_DIM // 2, 1)
    partner = jnp.where((lane & (HEAD_DIM // 2)) == 0, down, up)
    return t * cos + partner * sin_signed


def _ffn_proj_kernel(x_ref, g1_ref, w1_ref, w3_ref, w2_ref, gm_ref, w_ref, freq_ref,
                     x1_ref, qs_ref, ks_ref, vs_ref, qd_ref, kd_ref, vd_ref, act_ref, *, pos_blocks):
    half = HEAD_DIM // 2
    reps = LANES // half
    rows = TOKEN_TILE // reps
    first = (pl.program_id(0) % pos_blocks) * TOKEN_TILE
    grid_pos = (first + lax.broadcasted_iota(jnp.int32, (rows, LANES), 0)
                + (lax.broadcasted_iota(jnp.int32, (rows, LANES), 1) // half) * rows)
    ang = grid_pos.astype(F32) * freq_ref[...]
    cos_grid, sin_grid = jnp.cos(ang), jnp.sin(ang)
    cos_blocks, sin_blocks = [], []
    for k in range(reps):
        c = cos_grid[:, k * half:(k + 1) * half]
        s = sin_grid[:, k * half:(k + 1) * half]
        cos_blocks.append(jnp.concatenate([c] * reps, axis=1))
        sin_blocks.append(jnp.concatenate([-s, s] * (reps // 2), axis=1))
    cos = jnp.concatenate(cos_blocks, axis=0)
    sin = jnp.concatenate(sin_blocks, axis=0)

    x1 = _swiglu_half_step(x_ref[...], g1_ref[...], w1_ref, w3_ref, w2_ref, act_ref)
    x1_ref[...] = x1
    h = _rmsnorm(x1, gm_ref[...]).astype(BF16)

    def cols(lo, width):
        return jnp.dot(h, w_ref[:, lo:lo + width], preferred_element_type=F32)

    base = 3 * SB_W
    qd = cols(base, DQ_W)
    kd = cols(base + DQ_W, DKV_W)
    for c in range(DQ_W // LANES):
        sl = slice(c * LANES, (c + 1) * LANES)
        qd_ref[:, sl] = _rope_cols(qd[:, sl], cos, sin) * SCALE
    for c in range(DKV_W // LANES):
        sl = slice(c * LANES, (c + 1) * LANES)
        kd_ref[:, sl] = _rope_cols(kd[:, sl], cos, sin)
    qs_ref[...] = (cols(0, SB_W) * SCALE).astype(BF16)
    ks_ref[...] = cols(SB_W, SB_W).astype(BF16)
    vs_ref[...] = cols(2 * SB_W, SB_W).astype(BF16)
    vd_ref[...] = cols(base + DQ_W + DKV_W, DKV_W)


def _ffn_proj(x, g1, w1, w3, w2, g_mix, w_in, lane_freq, seq):
    n = x.shape[0]
    outs = ((D_MODEL, F32), (SB_W, BF16), (SB_W, BF16), (SB_W, BF16), (DQ_W, F32), (DKV_W, F32), (DKV_W, F32))
    return pl.pallas_call(
        functools.partial(_ffn_proj_kernel, pos_blocks=seq // TOKEN_TILE),
        out_shape=[jax.ShapeDtypeStruct((n, w), dt) for w, dt in outs],
        grid=(n // TOKEN_TILE,),
        in_specs=[_token_tile(D_MODEL), _resident((1, D_MODEL))] + _ffn_weight_specs()
                 + [_resident((1, D_MODEL)), _resident_cols(D_MODEL, 0, QKV_W), _resident((1, LANES))],
        out_specs=[_token_tile(w) for w, _ in outs],
        scratch_shapes=[pltpu.VMEM((TOKEN_TILE, D_FF), BF16)],
        compiler_params=_params(("parallel",)),
        name="ffn_proj",
    )(x, g1, w1, w3, w2, g_mix, w_in, lane_freq)


def _split2(x):
    p0 = x.astype(BF16)
    p1 = (x - p0.astype(F32)).astype(BF16)
    return jnp.concatenate([p0, p1], axis=1)


def _log_terms(z):
    log_not = -(jnp.maximum(z, 0.0) + jnp.log(1.0 + jnp.exp(-jnp.abs(z))))
    return log_not, z + log_not


def _sb_kernel(q_ref, k_ref, v_ref, tri2_ref, tri_ref, o_ref, acc_ref, carry_ref):
    step = pl.program_id(2)
    lane = lax.broadcasted_iota(jnp.int32, (SB_BLOCK, LANES), 1)
    head_lanes = (lane < HEAD_DIM, lane >= HEAD_DIM)
    ahead = (lax.broadcasted_iota(jnp.int32, (SB_BLOCK, 2 * SB_BLOCK), 1)
             - lax.broadcasted_iota(jnp.int32, (SB_BLOCK, 2 * SB_BLOCK), 0))

    older, starts, stricts = [], [], []
    for g in range(SB_GROUP):
        blk = step * SB_GROUP + g
        first = jnp.maximum(blk - 1, 0)
        starts.append(pl.multiple_of(first * SB_BLOCK, SB_BLOCK))
        stricts.append(ahead < (blk - first) * SB_BLOCK)
        older.append(first - 1)
    pairs = [(g, hd) for g in range(SB_GROUP) for hd in range(2)]
    zs = []
    for g, hd in pairs:
        q = q_ref[g * SB_BLOCK:(g + 1) * SB_BLOCK, :]
        qh = jnp.where(head_lanes[hd], q, jnp.zeros_like(q))
        kwin = k_ref[pl.ds(starts[g], 2 * SB_BLOCK), :]
        zs.append(lax.dot_general(qh, kwin, _NT, preferred_element_type=F32))
    log_betas, pieces, tops = [], [], []
    for (g, hd), z in zip(pairs, zs):
        log_not, log_beta = _log_terms(jnp.where(stricts[g], z, NEG_BIG))
        log_betas.append(log_beta)
        pieces.append(_split2(log_not))
        total = jnp.sum(log_not, axis=1, keepdims=True)
        carry_ref[g, hd] = jnp.broadcast_to(total, (SB_BLOCK, LANES))
        tops.append(jnp.max(total))
    betweens = [jnp.dot(p, tri2_ref[...], preferred_element_type=F32) for p in pieces]
    weights = [jnp.exp(lb + bt).astype(BF16) for lb, bt in zip(log_betas, betweens)]
    for (g, hd), a in zip(pairs, weights):
        vwin = v_ref[pl.ds(starts[g], 2 * SB_BLOCK), :]
        acc_ref[g, hd] = jnp.dot(a, vwin, preferred_element_type=F32)
    go = [(jnp.maximum(tops[2 * g], tops[2 * g + 1]) > SB_LOG_FLOOR).astype(jnp.int32)
          for g in range(SB_GROUP)]

    for g in range(SB_GROUP):

        def visit(state, g=g):
            kb, _ = state
            q = q_ref[g * SB_BLOCK:(g + 1) * SB_BLOCK, :]
            at = pl.multiple_of(kb * SB_BLOCK, SB_BLOCK)
            kblk = k_ref[pl.ds(at, SB_BLOCK), :]
            vblk = v_ref[pl.ds(at, SB_BLOCK), :]
            worst = None
            for hd in range(2):
                qh = jnp.where(head_lanes[hd], q, jnp.zeros_like(q))
                z = lax.dot_general(qh, kblk, _NT, preferred_element_type=F32)
                log_not, log_beta = _log_terms(z)
                sums = jnp.dot(_split2(log_not), tri_ref[...], preferred_element_type=F32)
                carry = carry_ref[g, hd]
                a = jnp.exp(log_beta + sums[:, :SB_BLOCK] + carry)
                acc_ref[g, hd] += jnp.dot(a.astype(BF16), vblk, preferred_element_type=F32)
                carry = carry + sums[:, SB_BLOCK:]
                carry_ref[g, hd] = carry
                top = jnp.max(carry)
                worst = top if worst is None else jnp.maximum(worst, top)
            return kb - 1, (worst > SB_LOG_FLOOR).astype(jnp.int32)

        def live(state):
            kb, more = state
            return jnp.logical_and(kb >= 0, more > 0)

        lax.while_loop(live, visit, (older[g], go[g]))
        o_ref[g * SB_BLOCK:(g + 1) * SB_BLOCK, :] = jnp.where(
            head_lanes[0], acc_ref[g, 0], acc_ref[g, 1]).astype(o_ref.dtype)


def _suffix_matrix(n):
    j = lax.broadcasted_iota(jnp.int32, (n, n), 0)
    s = lax.broadcasted_iota(jnp.int32, (n, n), 1)
    return (j > s).astype(BF16)


def _sb_attention(q, k, v, batch, seq):
    q3, k3, v3 = (t.reshape(batch, seq, SB_W) for t in (q, k, v))
    tri2 = jnp.tile(_suffix_matrix(2 * SB_BLOCK), (2, 1))
    tri = jnp.tile(jnp.concatenate([_suffix_matrix(SB_BLOCK), jnp.ones((SB_BLOCK, LANES), BF16)], axis=1), (2, 1))
    rows = SB_GROUP * SB_BLOCK
    qblk = pl.BlockSpec((None, rows, LANES), lambda b, p, i: (b, i, p))
    whole = pl.BlockSpec((None, seq, LANES), lambda b, p, i: (b, 0, p))
    scratch = pltpu.VMEM((SB_GROUP, 2, SB_BLOCK, LANES), F32)
    out = pl.pallas_call(
        _sb_kernel,
        out_shape=jax.ShapeDtypeStruct((batch, seq, SB_W), BF16),
        grid=(batch, SB_W // LANES, seq // rows),
        in_specs=[qblk, whole, whole, _resident(tri2.shape), _resident(tri.shape)],
        out_specs=qblk,
        scratch_shapes=[scratch, scratch],
        compiler_params=_params(("parallel", "parallel", "arbitrary")),
        name="sb_attn",
    )(q3, k3, v3, tri2, tri)
    return out.reshape(batch * seq, SB_W)


def _strided(start, size, stride):
    return pl.ds(start, size) if stride == 1 else pl.ds(start, size, stride=stride)


def _band_attention(blocks, head_lanes):
    scores = []
    for q, kcat, _, _ in blocks:
        for hd in range(2):
            qh = jnp.where(head_lanes[hd], q, jnp.zeros_like(q))
            scores.append(lax.dot_general(qh, kcat, _NT, preferred_element_type=F32))
    probs, stats = [], []
    for i, s in enumerate(scores):
        s = jnp.where(blocks[i // 2][3], s, NEG_BIG)
        mx = jnp.max(s, axis=-1, keepdims=True)
        e = jnp.exp(s - mx)
        probs.append(e.astype(BF16))
        stats.append((mx, jnp.sum(e, axis=-1, keepdims=True)))
    results = []
    for b, (_, _, vcat, _) in enumerate(blocks):
        outs, lses = [], []
        for hd in range(2):
            mx, den = stats[2 * b + hd]
            o = jnp.dot(probs[2 * b + hd], vcat, preferred_element_type=F32)
            outs.append(o / den)
            lses.append(jnp.broadcast_to(mx + jnp.log(den), (DIL_BAND, LANES)))
        results.append((jnp.where(head_lanes[0], outs[0], outs[1]),
                        jnp.where(head_lanes[0], lses[0], lses[1])))
    return results


def _dil_kernel(q0_ref, q1_ref, q2_ref, kprev_ref, k_ref, vprev_ref, v_ref, o_ref, obuf, lbuf, staged):
    step = pl.program_id(2)
    row = lax.broadcasted_iota(jnp.int32, (DIL_BAND, 2 * DIL_BAND), 0)
    col = lax.broadcasted_iota(jnp.int32, (DIL_BAND, 2 * DIL_BAND), 1)
    band = jnp.logical_and(col >= row, col <= row + DIL_BAND)
    band_first = jnp.logical_and(band, jnp.logical_or(col >= DIL_BAND, step > 0))
    lane = lax.broadcasted_iota(jnp.int32, (DIL_BAND, LANES), 1)
    head_lanes = (lane < HEAD_DIM, lane >= HEAD_DIM)
    q_refs = (q0_ref, q1_ref, q2_ref)
    part = DIL_WINDOW // DIL_SPLIT

    def operands(g):
        return q_refs[g], kprev_ref, k_ref, vprev_ref, v_ref

    def fetch(g, dil, which, r, m0, n):
        if isinstance(r, tuple):
            lo, hi = r
            inner = dil // DIL_SPLIT
            return staged[which, _strided(lo * part + hi + inner * m0, n, inner), :]
        return operands(g)[which][_strided(r + dil * m0, n, dil), :]

    def load(g, dil, r, j):
        m0 = DIL_BAND * j
        if dil == 1 and not isinstance(j, int):
            m0 = pl.multiple_of(m0, DIL_BAND)
        token_r = r[0] + DIL_SPLIT * r[1] if isinstance(r, tuple) else r
        q = fetch(g, dil, 0, r, m0, DIL_BAND).astype(BF16)
        if isinstance(j, int) and j == 0:
            tail = DIL_WINDOW // dil - DIL_BAND
            kcat = jnp.concatenate([fetch(g, dil, 1, r, tail, DIL_BAND), fetch(g, dil, 2, r, 0, DIL_BAND)], axis=0)
            vcat = jnp.concatenate([fetch(g, dil, 3, r, tail, DIL_BAND), fetch(g, dil, 4, r, 0, DIL_BAND)], axis=0)
            valid = band_first
        else:
            kcat = fetch(g, dil, 2, r, m0 - DIL_BAND, 2 * DIL_BAND)
            vcat = fetch(g, dil, 4, r, m0 - DIL_BAND, 2 * DIL_BAND)
            valid = band
        return token_r + dil * m0, (q, kcat.astype(BF16), vcat.astype(BF16), valid)

    def run(g, dil, where):
        loaded = [load(g, dil, r, j) for r, j in where]
        results = _band_attention([blk for _, blk in loaded], head_lanes)
        for (base, _), (out, lse) in zip(loaded, results):
            obuf[g, _strided(base, DIL_BAND, dil), :] = out
            lbuf[g, _strided(base, DIL_BAND, dil), :] = lse

    for g, (_, dil) in enumerate(DIL_GROUPS):
        per_residue = DIL_WINDOW // (dil * DIL_BAND)
        if dil == 1:
            run(g, dil, [(0, j) for j in range(DIL_BATCH)])

            def body(i, _, g=g, dil=dil):
                run(g, dil, [(0, i * DIL_BATCH + j) for j in range(DIL_BATCH)])
                return 0

            lax.fori_loop(1, per_residue // DIL_BATCH, body, 0)
        elif dil <= DIL_SPLIT:
            def body(r, _, g=g, dil=dil, per_residue=per_residue):
                for j0 in range(0, per_residue, DIL_BATCH):
                    run(g, dil, [(r, j) for j in range(j0, min(j0 + DIL_BATCH, per_residue))])
                return 0

            lax.fori_loop(0, dil, body, 0)
        else:
            for which, ref in enumerate(operands(g)):
                for lo in range(DIL_SPLIT):
                    staged[which, lo * part:(lo + 1) * part, :] = ref[pl.ds(lo, part, stride=DIL_SPLIT), :]

            def body(hi, _, g=g, dil=dil, per_residue=per_residue):
                run(g, dil, [((lo, hi), j) for lo in range(DIL_SPLIT) for j in range(per_residue)])
                return 0

            lax.fori_loop(0, dil // DIL_SPLIT, body, 0)

    chunk = 2 * DIL_BAND
    for c in range(DIL_WINDOW // chunk):
        rows = slice(c * chunk, (c + 1) * chunk)
        lses = [lbuf[g, rows, :] for g in range(len(DIL_GROUPS))]
        top = functools.reduce(jnp.maximum, lses)
        ws = [jnp.exp(l - top) for l in lses]
        num = sum(w * obuf[g, rows, :] for g, w in enumerate(ws))
        o_ref[rows, :] = (num / sum(ws)).astype(o_ref.dtype)


def _dil_attention(q, k, v, batch, seq):
    n_groups = len(DIL_GROUPS)
    pairs = DKV_W // LANES
    q3 = q.reshape(batch, seq, DQ_W)
    k3 = k.reshape(batch, seq, DKV_W)
    v3 = v.reshape(batch, seq, DKV_W)

    def q_spec(g):
        return pl.BlockSpec((None, DIL_WINDOW, LANES), lambda b, p, i: (b, i, g * pairs + p))

    cur = pl.BlockSpec((None, DIL_WINDOW, LANES), lambda b, p, i: (b, i, p))
    prev = pl.BlockSpec((None, DIL_WINDOW, LANES), lambda b, p, i: (b, jnp.maximum(i - 1, 0), p))
    buf = pltpu.VMEM((n_groups, DIL_WINDOW, LANES), F32)
    staged = pltpu.VMEM((5, DIL_WINDOW, LANES), F32)
    out = pl.pallas_call(
        _dil_kernel,
        out_shape=jax.ShapeDtypeStruct((batch, seq, DKV_W), BF16),
        grid=(batch, pairs, seq // DIL_WINDOW),
        in_specs=[q_spec(g) for g in range(n_groups)] + [prev, cur, prev, cur],
        out_specs=cur,
        scratch_shapes=[buf, buf, staged],
        compiler_params=_params(("parallel", "parallel", "arbitrary")),
        name="dil_attn",
    )(q3, q3, q3, k3, k3, v3, v3)
    return out.reshape(batch * seq, DKV_W)


def _mix_ffn_kernel(*refs, final_norm):
    (x_ref, gm_ref, wg_ref, bg_ref, osb_ref, odl_ref, wsb_ref, wdl_ref, wo_ref,
     g2_ref, w1_ref, w3_ref, w2_ref) = refs[:13]
    gf_ref = refs[13] if final_norm else None
    o_ref, act_ref = refs[-2:]
    x = x_ref[...]
    h = _rmsnorm(x, gm_ref[...]).astype(BF16)

    def gate(lo):
        pre = jnp.dot(h, wg_ref[:, lo:lo + D_MODEL], preferred_element_type=F32)
        return jax.nn.sigmoid(pre + bg_ref[:, lo:lo + D_MODEL])

    mixed = gate(0) * jnp.dot(osb_ref[...], wsb_ref[...], preferred_element_type=F32)
    mixed = mixed + gate(D_MODEL) * jnp.dot(odl_ref[...], wdl_ref[...], preferred_element_type=F32)
    x2 = x + jnp.dot(mixed.astype(BF16), wo_ref[...], preferred_element_type=F32)
    y = _swiglu_half_step(x2, g2_ref[...], w1_ref, w3_ref, w2_ref, act_ref)
    if final_norm:
        y = _rmsnorm(y, gf_ref[...])
    o_ref[...] = y


def _mix_ffn(x, g_mix, w_in, b_gate, o_sb, o_dl, w_sb_out, w_dil_out, w_o, g2, w1, w3, w2, g_final=None):
    n = x.shape[0]
    in_specs = ([_token_tile(D_MODEL), _resident((1, D_MODEL)), _resident_cols(D_MODEL, QKV_W, 2 * D_MODEL),
                 _resident((1, 2 * D_MODEL)), _token_tile(SB_W), _token_tile(DKV_W),
                 _resident((SB_W, D_MODEL)), _resident((DKV_W, D_MODEL)), _resident((D_MODEL, D_MODEL)),
                 _resident((1, D_MODEL))] + _ffn_weight_specs())
    args = [x, g_mix, w_in, b_gate, o_sb, o_dl, w_sb_out, w_dil_out, w_o, g2, w1, w3, w2]
    if g_final is not None:
        in_specs.append(_resident((1, D_MODEL)))
        args.append(g_final)
    return pl.pallas_call(
        functools.partial(_mix_ffn_kernel, final_norm=g_final is not None),
        out_shape=jax.ShapeDtypeStruct((n, D_MODEL), F32),
        grid=(n // TOKEN_TILE,),
        in_specs=in_specs,
        out_specs=_token_tile(D_MODEL),
        scratch_shapes=[pltpu.VMEM((TOKEN_TILE, D_FF), BF16)],
        compiler_params=_params(("parallel",)),
        name="mix_ffn",
    )(*args)


def _rope_lane_freq():
    half = HEAD_DIM // 2
    inv_freq = ROPE_THETA ** (-jnp.arange(half, dtype=F32) / half)
    return jnp.tile(inv_freq, LANES // half).reshape(1, LANES)


def kernel(x, g_ffn1, w1_a, w3_a, w2_a, g_mix, w_in, b_gate, w_sb_out, w_dil_out, w_o,
           g_ffn2, w1_b, w3_b, w2_b, g_final):
    batch, seq, _ = x.shape
    depth = g_ffn1.shape[0]
    assert seq % DIL_WINDOW == 0 and seq % (SB_GROUP * SB_BLOCK) == 0 and seq % TOKEN_TILE == 0
    assert all(w == DIL_BAND * d and DIL_WINDOW % (DIL_BAND * d) == 0 for w, d in DIL_GROUPS)
    assert all(d <= DIL_SPLIT or (d % DIL_SPLIT == 0 and d // DIL_SPLIT <= DIL_SPLIT) for _, d in DIL_GROUPS)
    lane_freq = _rope_lane_freq()
    row = lambda t: t.reshape(1, -1)
    bf = lambda t: t.astype(BF16)
    xt = x.reshape(batch * seq, D_MODEL)
    for l in range(depth):
        w_in_l = bf(w_in[l])
        xt, q_sb, k_sb, v_sb, q_dl, k_dl, v_dl = _ffn_proj(
            xt, row(g_ffn1[l]), bf(w1_a[l]), bf(w3_a[l]), bf(w2_a[l]),
            row(g_mix[l]), w_in_l, lane_freq, seq)
        o_sb = _sb_attention(q_sb, k_sb, v_sb, batch, seq)
        o_dl = _dil_attention(q_dl, k_dl, v_dl, batch, seq)
        xt = _mix_ffn(xt, row(g_mix[l]), w_in_l, row(b_gate[l]), o_sb, o_dl,
                      bf(w_sb_out[l]), bf(w_dil_out[l]), bf(w_o[l]),
                      row(g_ffn2[l]), bf(w1_b[l]), bf(w3_b[l]), bf(w2_b[l]),
                      row(g_final) if l == depth - 1 else None)
    return xt.reshape(batch, seq, D_MODEL)
```

```python
import functools

import jax
import jax.numpy as jnp
from jax import lax
from jax.experimental import pallas as pl
from jax.experimental.pallas import tpu as pltpu

D_MODEL = 1024
HEAD_DIM = 64
SB_HEADS = 8
DIL_GROUPS = ((128, 1), (512, 4), (2048, 16))
DIL_KV_HEADS = 4
D_FF = 2816
ROPE_THETA = 10000.0
RMS_EPS = 1e-6

SB_W = SB_HEADS * HEAD_DIM
DKV_W = DIL_KV_HEADS * HEAD_DIM
DQ_W = DKV_W * len(DIL_GROUPS)
QKV_W = 3 * SB_W + DQ_W + 2 * DKV_W
SCALE = HEAD_DIM ** -0.5

LANES = 128
TOKEN_TILE = 512
FF_CHUNK = 256
SB_BLOCK = 128
SB_GROUP = 16
DIL_BAND = 128
DIL_WINDOW = DIL_BAND * max(d for _, d in DIL_GROUPS)
DIL_BATCH = 4
DIL_SPLIT = 4
SB_SPENT_ENOUGH = 88.0
NEG_BIG = -1e30
LOG2_E = 1.4426950408889634
VMEM_LIMIT = 56 * 1024 * 1024

F32 = jnp.float32
BF16 = jnp.bfloat16
_NT = (((1,), (1,)), ((), ()))


def _rmsnorm(x, g):
    ms = jnp.mean(x * x, axis=-1, keepdims=True)
    return x * lax.rsqrt(ms + RMS_EPS) * g


def _resident(shape):
    return pl.BlockSpec(shape, lambda *_: (0,) * len(shape), pipeline_mode=pl.Buffered(1))


def _resident_cols(rows, lo, width):
    return pl.BlockSpec((pl.Element(rows), pl.Element(width)), lambda *_: (0, lo),
                        pipeline_mode=pl.Buffered(1))


def _params(semantics):
    return pltpu.CompilerParams(dimension_semantics=semantics, vmem_limit_bytes=VMEM_LIMIT)


def _swiglu_half_step(x, g, w1_ref, w3_ref, w2_ref, act_ref):
    h = _rmsnorm(x, g).astype(BF16)
    for lo in range(0, D_FF, FF_CHUNK):
        sl = slice(lo, lo + FF_CHUNK)
        u = jnp.dot(h, w1_ref[:, sl], preferred_element_type=F32)
        v = jnp.dot(h, w3_ref[:, sl], preferred_element_type=F32)
        act_ref[:, sl] = (u * jax.nn.sigmoid(u) * v).astype(BF16)
    return x + 0.5 * jnp.dot(act_ref[...], w2_ref[...], preferred_element_type=F32)


def _ffn_weight_specs():
    return [_resident((D_MODEL, D_FF)), _resident((D_MODEL, D_FF)), _resident((D_FF, D_MODEL))]


def _token_tile(width):
    return pl.BlockSpec((TOKEN_TILE, width), lambda i: (i, 0))


def _rope_cols(t, cos, sin_signed):
    lane = lax.broadcasted_iota(jnp.int32, t.shape, 1)
    up = pltpu.roll(t, HEAD_DIM // 2, 1)
    down = pltpu.roll(t, LANES - HEAD_DIM // 2, 1)
    partner = jnp.where((lane & (HEAD_DIM // 2)) == 0, down, up)
    return t * cos + partner * sin_signed


def _ffn_proj_kernel(x_ref, g1_ref, w1_ref, w3_ref, w2_ref, gm_ref, w_ref, freq_ref,
                     x1_ref, qs_ref, ks_ref, vs_ref, qd_ref, kd_ref, vd_ref, act_ref, *, pos_blocks):
    half = HEAD_DIM // 2
    reps = LANES // half
    rows = TOKEN_TILE // reps
    first = (pl.program_id(0) % pos_blocks) * TOKEN_TILE
    grid_pos = (first + lax.broadcasted_iota(jnp.int32, (rows, LANES), 0)
                + (lax.broadcasted_iota(jnp.int32, (rows, LANES), 1) // half) * rows)
    ang = grid_pos.astype(F32) * freq_ref[...]
    cos_grid, sin_grid = jnp.cos(ang), jnp.sin(ang)
    cos_blocks, sin_blocks = [], []
    for k in range(reps):
        c = cos_grid[:, k * half:(k + 1) * half]
        s = sin_grid[:, k * half:(k + 1) * half]
        cos_blocks.append(jnp.concatenate([c] * reps, axis=1))
        sin_blocks.append(jnp.concatenate([-s, s] * (reps // 2), axis=1))
    cos = jnp.concatenate(cos_blocks, axis=0)
    sin = jnp.concatenate(sin_blocks, axis=0)

    x1 = _swiglu_half_step(x_ref[...], g1_ref[...], w1_ref, w3_ref, w2_ref, act_ref)
    x1_ref[...] = x1
    h = _rmsnorm(x1, gm_ref[...]).astype(BF16)

    def cols(lo, width):
        return jnp.dot(h, w_ref[:, lo:lo + width], preferred_element_type=F32)

    base = 3 * SB_W
    qd = cols(base, DQ_W)
    kd = cols(base + DQ_W, DKV_W)
    for c in range(DQ_W // LANES):
        sl = slice(c * LANES, (c + 1) * LANES)
        qd_ref[:, sl] = _rope_cols(qd[:, sl], cos, sin) * SCALE
    for c in range(DKV_W // LANES):
        sl = slice(c * LANES, (c + 1) * LANES)
        kd_ref[:, sl] = _rope_cols(kd[:, sl], cos, sin)
    qs_ref[...] = (cols(0, SB_W) * SCALE).astype(BF16)
    ks_ref[...] = cols(SB_W, SB_W).astype(BF16)
    vs_ref[...] = cols(2 * SB_W, SB_W).astype(BF16)
    vd_ref[...] = cols(base + DQ_W + DKV_W, DKV_W)


def _ffn_proj(x, g1, w1, w3, w2, g_mix, w_in, lane_freq, seq):
    n = x.shape[0]
    outs = ((D_MODEL, F32), (SB_W, BF16), (SB_W, BF16), (SB_W, BF16), (DQ_W, F32), (DKV_W, F32), (DKV_W, F32))
    return pl.pallas_call(
        functools.partial(_ffn_proj_kernel, pos_blocks=seq // TOKEN_TILE),
        out_shape=[jax.ShapeDtypeStruct((n, w), dt) for w, dt in outs],
        grid=(n // TOKEN_TILE,),
        in_specs=[_token_tile(D_MODEL), _resident((1, D_MODEL))] + _ffn_weight_specs()
                 + [_resident((1, D_MODEL)), _resident_cols(D_MODEL, 0, QKV_W), _resident((1, LANES))],
        out_specs=[_token_tile(w) for w, _ in outs],
        scratch_shapes=[pltpu.VMEM((TOKEN_TILE, D_FF), BF16)],
        compiler_params=_params(("parallel",)),
        name="ffn_proj",
    )(x, g1, w1, w3, w2, g_mix, w_in, lane_freq)


def _split2(x):
    p0 = x.astype(BF16)
    p1 = (x - p0.astype(F32)).astype(BF16)
    return jnp.concatenate([p0, p1], axis=1)


def _softplus(z):
    return jnp.maximum(z, 0.0) + jnp.log(1.0 + jnp.exp2(jnp.abs(z) * -LOG2_E))


def _sb_kernel(q_ref, k_ref, v_ref, tri2_ref, tri_ref, o_ref, acc_ref, spent_ref):
    step = pl.program_id(2)
    lane = lax.broadcasted_iota(jnp.int32, (SB_BLOCK, LANES), 1)
    head_lanes = (lane < HEAD_DIM, lane >= HEAD_DIM)
    ahead = (lax.broadcasted_iota(jnp.int32, (SB_BLOCK, 2 * SB_BLOCK), 1)
             - lax.broadcasted_iota(jnp.int32, (SB_BLOCK, 2 * SB_BLOCK), 0))

    older, starts, stricts = [], [], []
    for g in range(SB_GROUP):
        blk = step * SB_GROUP + g
        first = jnp.maximum(blk - 1, 0)
        starts.append(pl.multiple_of(first * SB_BLOCK, SB_BLOCK))
        stricts.append(ahead < (blk - first) * SB_BLOCK)
        older.append(first - 1)
    pairs = [(g, hd) for g in range(SB_GROUP) for hd in range(2)]
    zs = []
    for g, hd in pairs:
        q = q_ref[g * SB_BLOCK:(g + 1) * SB_BLOCK, :]
        qh = jnp.where(head_lanes[hd], q, jnp.zeros_like(q))
        kwin = k_ref[pl.ds(starts[g], 2 * SB_BLOCK), :]
        zs.append(lax.dot_general(qh, kwin, _NT, preferred_element_type=F32))
    log_betas, pieces, least = [], [], []
    for (g, hd), z in zip(pairs, zs):
        z = jnp.where(stricts[g], z, NEG_BIG)
        cost = _softplus(z)
        log_betas.append(z - cost)
        pieces.append(_split2(cost))
        total = jnp.sum(cost, axis=1, keepdims=True)
        spent_ref[g, hd] = jnp.broadcast_to(total, (SB_BLOCK, LANES))
        least.append(jnp.min(total))
    betweens = [jnp.dot(p, tri2_ref[...], preferred_element_type=F32) for p in pieces]
    weights = [jnp.exp(lb - bt).astype(BF16) for lb, bt in zip(log_betas, betweens)]
    for (g, hd), a in zip(pairs, weights):
        vwin = v_ref[pl.ds(starts[g], 2 * SB_BLOCK), :]
        acc_ref[g, hd] = jnp.dot(a, vwin, preferred_element_type=F32)
    go = [(jnp.minimum(least[2 * g], least[2 * g + 1]) < SB_SPENT_ENOUGH).astype(jnp.int32)
          for g in range(SB_GROUP)]

    for g in range(SB_GROUP):

        def visit(state, g=g):
            kb, _ = state
            q = q_ref[g * SB_BLOCK:(g + 1) * SB_BLOCK, :]
            at = pl.multiple_of(kb * SB_BLOCK, SB_BLOCK)
            kblk = k_ref[pl.ds(at, SB_BLOCK), :]
            vblk = v_ref[pl.ds(at, SB_BLOCK), :]
            lowest = None
            for hd in range(2):
                qh = jnp.where(head_lanes[hd], q, jnp.zeros_like(q))
                z = lax.dot_general(qh, kblk, _NT, preferred_element_type=F32)
                cost = _softplus(z)
                sums = jnp.dot(_split2(cost), tri_ref[...], preferred_element_type=F32)
                spent = spent_ref[g, hd]
                a = jnp.exp(z - cost - sums[:, :SB_BLOCK] - spent)
                acc_ref[g, hd] += jnp.dot(a.astype(BF16), vblk, preferred_element_type=F32)
                spent = spent + sums[:, SB_BLOCK:]
                spent_ref[g, hd] = spent
                low = jnp.min(spent)
                lowest = low if lowest is None else jnp.minimum(lowest, low)
            return kb - 1, (lowest < SB_SPENT_ENOUGH).astype(jnp.int32)

        def live(state):
            kb, more = state
            return jnp.logical_and(kb >= 0, more > 0)

        lax.while_loop(live, visit, (older[g], go[g]))
        o_ref[g * SB_BLOCK:(g + 1) * SB_BLOCK, :] = jnp.where(
            head_lanes[0], acc_ref[g, 0], acc_ref[g, 1]).astype(o_ref.dtype)


def _suffix_matrix(n):
    j = lax.broadcasted_iota(jnp.int32, (n, n), 0)
    s = lax.broadcasted_iota(jnp.int32, (n, n), 1)
    return (j > s).astype(BF16)


def _sb_attention(q, k, v, batch, seq):
    q3, k3, v3 = (t.reshape(batch, seq, SB_W) for t in (q, k, v))
    tri2 = jnp.tile(_suffix_matrix(2 * SB_BLOCK), (2, 1))
    tri = jnp.tile(jnp.concatenate([_suffix_matrix(SB_BLOCK), jnp.ones((SB_BLOCK, LANES), BF16)], axis=1), (2, 1))
    rows = SB_GROUP * SB_BLOCK
    qblk = pl.BlockSpec((None, rows, LANES), lambda b, p, i: (b, i, p))
    whole = pl.BlockSpec((None, seq, LANES), lambda b, p, i: (b, 0, p))
    scratch = pltpu.VMEM((SB_GROUP, 2, SB_BLOCK, LANES), F32)
    out = pl.pallas_call(
        _sb_kernel,
        out_shape=jax.ShapeDtypeStruct((batch, seq, SB_W), BF16),
        grid=(batch, SB_W // LANES, seq // rows),
        in_specs=[qblk, whole, whole, _resident(tri2.shape), _resident(tri.shape)],
        out_specs=qblk,
        scratch_shapes=[scratch, scratch],
        compiler_params=_params(("parallel", "parallel", "arbitrary")),
        name="sb_attn",
    )(q3, k3, v3, tri2, tri)
    return out.reshape(batch * seq, SB_W)


def _strided(start, size, stride):
    return pl.ds(start, size) if stride == 1 else pl.ds(start, size, stride=stride)


def _band_attention(blocks, head_lanes):
    scores = []
    for q, kcat, _, _ in blocks:
        for hd in range(2):
            qh = jnp.where(head_lanes[hd], q, jnp.zeros_like(q))
            scores.append(lax.dot_general(qh, kcat, _NT, preferred_element_type=F32))
    probs, stats = [], []
    for i, s in enumerate(scores):
        s = jnp.where(blocks[i // 2][3], s, NEG_BIG)
        mx = jnp.max(s, axis=-1, keepdims=True)
        e = jnp.exp(s - mx)
        probs.append(e.astype(BF16))
        stats.append((mx, jnp.sum(e, axis=-1, keepdims=True)))
    results = []
    for b, (_, _, vcat, _) in enumerate(blocks):
        outs, lses = [], []
        for hd in range(2):
            mx, den = stats[2 * b + hd]
            o = jnp.dot(probs[2 * b + hd], vcat, preferred_element_type=F32)
            outs.append(o / den)
            lses.append(jnp.broadcast_to(mx + jnp.log(den), (DIL_BAND, LANES)))
        results.append((jnp.where(head_lanes[0], outs[0], outs[1]),
                        jnp.where(head_lanes[0], lses[0], lses[1])))
    return results


def _dil_kernel(q0_ref, q1_ref, q2_ref, kprev_ref, k_ref, vprev_ref, v_ref, o_ref, obuf, lbuf, staged):
    step = pl.program_id(2)
    row = lax.broadcasted_iota(jnp.int32, (DIL_BAND, 2 * DIL_BAND), 0)
    col = lax.broadcasted_iota(jnp.int32, (DIL_BAND, 2 * DIL_BAND), 1)
    band = jnp.logical_and(col >= row, col <= row + DIL_BAND)
    band_first = jnp.logical_and(band, jnp.logical_or(col >= DIL_BAND, step > 0))
    lane = lax.broadcasted_iota(jnp.int32, (DIL_BAND, LANES), 1)
    head_lanes = (lane < HEAD_DIM, lane >= HEAD_DIM)
    q_refs = (q0_ref, q1_ref, q2_ref)
    part = DIL_WINDOW // DIL_SPLIT

    def operands(g):
        return q_refs[g], kprev_ref, k_ref, vprev_ref, v_ref

    def fetch(g, dil, which, r, m0, n):
        if isinstance(r, tuple):
            lo, hi = r
            inner = dil // DIL_SPLIT
            return staged[which, _strided(lo * part + hi + inner * m0, n, inner), :]
        return operands(g)[which][_strided(r + dil * m0, n, dil), :]

    def load(g, dil, r, j):
        m0 = DIL_BAND * j
        if dil == 1 and not isinstance(j, int):
            m0 = pl.multiple_of(m0, DIL_BAND)
        token_r = r[0] + DIL_SPLIT * r[1] if isinstance(r, tuple) else r
        q = fetch(g, dil, 0, r, m0, DIL_BAND).astype(BF16)
        if isinstance(j, int) and j == 0:
            tail = DIL_WINDOW // dil - DIL_BAND
            kcat = jnp.concatenate([fetch(g, dil, 1, r, tail, DIL_BAND), fetch(g, dil, 2, r, 0, DIL_BAND)], axis=0)
            vcat = jnp.concatenate([fetch(g, dil, 3, r, tail, DIL_BAND), fetch(g, dil, 4, r, 0, DIL_BAND)], axis=0)
            valid = band_first
        else:
            kcat = fetch(g, dil, 2, r, m0 - DIL_BAND, 2 * DIL_BAND)
            vcat = fetch(g, dil, 4, r, m0 - DIL_BAND, 2 * DIL_BAND)
            valid = band
        return token_r + dil * m0, (q, kcat.astype(BF16), vcat.astype(BF16), valid)

    def run(g, dil, where):
        loaded = [load(g, dil, r, j) for r, j in where]
        results = _band_attention([blk for _, blk in loaded], head_lanes)
        for (base, _), (out, lse) in zip(loaded, results):
            obuf[g, _strided(base, DIL_BAND, dil), :] = out
            lbuf[g, _strided(base, DIL_BAND, dil), :] = lse

    for g, (_, dil) in enumerate(DIL_GROUPS):
        per_residue = DIL_WINDOW // (dil * DIL_BAND)
        if dil == 1:
            run(g, dil, [(0, j) for j in range(DIL_BATCH)])

            def body(i, _, g=g, dil=dil):
                run(g, dil, [(0, i * DIL_BATCH + j) for j in range(DIL_BATCH)])
                return 0

            lax.fori_loop(1, per_residue // DIL_BATCH, body, 0)
        elif dil <= DIL_SPLIT:
            def body(r, _, g=g, dil=dil, per_residue=per_residue):
                for j0 in range(0, per_residue, DIL_BATCH):
                    run(g, dil, [(r, j) for j in range(j0, min(j0 + DIL_BATCH, per_residue))])
                return 0

            lax.fori_loop(0, dil, body, 0)
        else:
            for which, ref in enumerate(operands(g)):
                for lo in range(DIL_SPLIT):
                    staged[which, lo * part:(lo + 1) * part, :] = ref[pl.ds(lo, part, stride=DIL_SPLIT), :]

            def body(hi, _, g=g, dil=dil, per_residue=per_residue):
                run(g, dil, [((lo, hi), j) for lo in range(DIL_SPLIT) for j in range(per_residue)])
                return 0

            lax.fori_loop(0, dil // DIL_SPLIT, body, 0)

    chunk = 2 * DIL_BAND
    for c in range(DIL_WINDOW // chunk):
        rows = slice(c * chunk, (c + 1) * chunk)
        lses = [lbuf[g, rows, :] for g in range(len(DIL_GROUPS))]
        top = functools.reduce(jnp.maximum, lses)
        ws = [jnp.exp(l - top) for l in lses]
        num = sum(w * obuf[g, rows, :] for g, w in enumerate(ws))
        o_ref[rows, :] = (num / sum(ws)).astype(o_ref.dtype)


def _dil_attention(q, k, v, batch, seq):
    n_groups = len(DIL_GROUPS)
    pairs = DKV_W // LANES
    q3 = q.reshape(batch, seq, DQ_W)
    k3 = k.reshape(batch, seq, DKV_W)
    v3 = v.reshape(batch, seq, DKV_W)

    def q_spec(g):
        return pl.BlockSpec((None, DIL_WINDOW, LANES), lambda b, p, i: (b, i, g * pairs + p))

    cur = pl.BlockSpec((None, DIL_WINDOW, LANES), lambda b, p, i: (b, i, p))
    prev = pl.BlockSpec((None, DIL_WINDOW, LANES), lambda b, p, i: (b, jnp.maximum(i - 1, 0), p))
    buf = pltpu.VMEM((n_groups, DIL_WINDOW, LANES), F32)
    staged = pltpu.VMEM((5, DIL_WINDOW, LANES), F32)
    out = pl.pallas_call(
        _dil_kernel,
        out_shape=jax.ShapeDtypeStruct((batch, seq, DKV_W), BF16),
        grid=(batch, pairs, seq // DIL_WINDOW),
        in_specs=[q_spec(g) for g in range(n_groups)] + [prev, cur, prev, cur],
        out_specs=cur,
        scratch_shapes=[buf, buf, staged],
        compiler_params=_params(("parallel", "parallel", "arbitrary")),
        name="dil_attn",
    )(q3, q3, q3, k3, k3, v3, v3)
    return out.reshape(batch * seq, DKV_W)


def _mix_ffn_kernel(*refs, final_norm):
    (x_ref, gm_ref, wg_ref, bg_ref, osb_ref, odl_ref, wsb_ref, wdl_ref, wo_ref,
     g2_ref, w1_ref, w3_ref, w2_ref) = refs[:13]
    gf_ref = refs[13] if final_norm else None
    o_ref, act_ref = refs[-2:]
    x = x_ref[...]
    h = _rmsnorm(x, gm_ref[...]).astype(BF16)

    def gate(lo):
        pre = jnp.dot(h, wg_ref[:, lo:lo + D_MODEL], preferred_element_type=F32)
        return jax.nn.sigmoid(pre + bg_ref[:, lo:lo + D_MODEL])

    mixed = gate(0) * jnp.dot(osb_ref[...], wsb_ref[...], preferred_element_type=F32)
    mixed = mixed + gate(D_MODEL) * jnp.dot(odl_ref[...], wdl_ref[...], preferred_element_type=F32)
    x2 = x + jnp.dot(mixed.astype(BF16), wo_ref[...], preferred_element_type=F32)
    y = _swiglu_half_step(x2, g2_ref[...], w1_ref, w3_ref, w2_ref, act_ref)
    if final_norm:
        y = _rmsnorm(y, gf_ref[...])
    o_ref[...] = y


def _mix_ffn(x, g_mix, w_in, b_gate, o_sb, o_dl, w_sb_out, w_dil_out, w_o, g2, w1, w3, w2, g_final=None):
    n = x.shape[0]
    in_specs = ([_token_tile(D_MODEL), _resident((1, D_MODEL)), _resident_cols(D_MODEL, QKV_W, 2 * D_MODEL),
                 _resident((1, 2 * D_MODEL)), _token_tile(SB_W), _token_tile(DKV_W),
                 _resident((SB_W, D_MODEL)), _resident((DKV_W, D_MODEL)), _resident((D_MODEL, D_MODEL)),
                 _resident((1, D_MODEL))] + _ffn_weight_specs())
    args = [x, g_mix, w_in, b_gate, o_sb, o_dl, w_sb_out, w_dil_out, w_o, g2, w1, w3, w2]
    if g_final is not None:
        in_specs.append(_resident((1, D_MODEL)))
        args.append(g_final)
    return pl.pallas_call(
        functools.partial(_mix_ffn_kernel, final_norm=g_final is not None),
        out_shape=jax.ShapeDtypeStruct((n, D_MODEL), F32),
        grid=(n // TOKEN_TILE,),
        in_specs=in_specs,
        out_specs=_token_tile(D_MODEL),
        scratch_shapes=[pltpu.VMEM((TOKEN_TILE, D_FF), BF16)],
        compiler_params=_params(("parallel",)),
        name="mix_ffn",
    )(*args)


def _rope_lane_freq():
    half = HEAD_DIM // 2
    inv_freq = ROPE_THETA ** (-jnp.arange(half, dtype=F32) / half)
    return jnp.tile(inv_freq, LANES // half).reshape(1, LANES)


def kernel(x, g_ffn1, w1_a, w3_a, w2_a, g_mix, w_in, b_gate, w_sb_out, w_dil_out, w_o,
           g_ffn2, w1_b, w3_b, w2_b, g_final):
    batch, seq, _ = x.shape
    depth = g_ffn1.shape[0]
    assert seq % DIL_WINDOW == 0 and seq % (SB_GROUP * SB_BLOCK) == 0 and seq % TOKEN_TILE == 0
    assert all(w == DIL_BAND * d and DIL_WINDOW % (DIL_BAND * d) == 0 for w, d in DIL_GROUPS)
    assert all(d <= DIL_SPLIT or (d % DIL_SPLIT == 0 and d // DIL_SPLIT <= DIL_SPLIT) for _, d in DIL_GROUPS)
    lane_freq = _rope_lane_freq()
    row = lambda t: t.reshape(1, -1)
    bf = lambda t: t.astype(BF16)
    xt = x.reshape(batch * seq, D_MODEL)
    for l in range(depth):
        w_in_l = bf(w_in[l])
        xt, q_sb, k_sb, v_sb, q_dl, k_dl, v_dl = _ffn_proj(
            xt, row(g_ffn1[l]), bf(w1_a[l]), bf(w3_a[l]), bf(w2_a[l]),
            row(g_mix[l]), w_in_l, lane_freq, seq)
        o_sb = _sb_attention(q_sb, k_sb, v_sb, batch, seq)
        o_dl = _dil_attention(q_dl, k_dl, v_dl, batch, seq)
        xt = _mix_ffn(xt, row(g_mix[l]), w_in_l, row(b_gate[l]), o_sb, o_dl,
                      bf(w_sb_out[l]), bf(w_dil_out[l]), bf(w_o[l]),
                      row(g_ffn2[l]), bf(w1_b[l]), bf(w3_b[l]), bf(w2_b[l]),
                      row(g_final) if l == depth - 1 else None)
    return xt.reshape(batch, seq, D_MODEL)
```

```python
import functools

import jax
import jax.numpy as jnp
from jax import lax
from jax.experimental import pallas as pl
from jax.experimental.pallas import tpu as pltpu

D_MODEL = 1024
HEAD_DIM = 64
SB_HEADS = 8
DIL_GROUPS = ((128, 1), (512, 4), (2048, 16))
DIL_KV_HEADS = 4
D_FF = 2816
ROPE_THETA = 10000.0
RMS_EPS = 1e-6

SB_W = SB_HEADS * HEAD_DIM
DKV_W = DIL_KV_HEADS * HEAD_DIM
DQ_W = DKV_W * len(DIL_GROUPS)
QKV_W = 3 * SB_W + DQ_W + 2 * DKV_W
SCALE = HEAD_DIM ** -0.5

LANES = 128
TOKEN_TILE = 512
FF_CHUNK = 256
SB_BLOCK = 128
SB_GROUP = 32
DIL_BAND = 128
DIL_WINDOW = DIL_BAND * max(d for _, d in DIL_GROUPS)
DIL_BATCH = 4
DIL_SPLIT = 4
SB_SPENT_ENOUGH = 88.0
NEG_BIG = -1e30
LOG2_E = 1.4426950408889634
VMEM_LIMIT = 56 * 1024 * 1024

F32 = jnp.float32
BF16 = jnp.bfloat16
_NT = (((1,), (1,)), ((), ()))


def _rmsnorm(x, g):
    ms = jnp.mean(x * x, axis=-1, keepdims=True)
    return x * lax.rsqrt(ms + RMS_EPS) * g


def _resident(shape):
    return pl.BlockSpec(shape, lambda *_: (0,) * len(shape), pipeline_mode=pl.Buffered(1))


def _resident_cols(rows, lo, width):
    return pl.BlockSpec((pl.Element(rows), pl.Element(width)), lambda *_: (0, lo),
                        pipeline_mode=pl.Buffered(1))


def _params(semantics):
    return pltpu.CompilerParams(dimension_semantics=semantics, vmem_limit_bytes=VMEM_LIMIT)


def _swiglu_half_step(x, g, w1_ref, w3_ref, w2_ref, act_ref):
    h = _rmsnorm(x, g).astype(BF16)
    for lo in range(0, D_FF, FF_CHUNK):
        sl = slice(lo, lo + FF_CHUNK)
        u = jnp.dot(h, w1_ref[:, sl], preferred_element_type=F32)
        v = jnp.dot(h, w3_ref[:, sl], preferred_element_type=F32)
        act_ref[:, sl] = (u * jax.nn.sigmoid(u) * v).astype(BF16)
    return x + 0.5 * jnp.dot(act_ref[...], w2_ref[...], preferred_element_type=F32)


def _ffn_weight_specs():
    return [_resident((D_MODEL, D_FF)), _resident((D_MODEL, D_FF)), _resident((D_FF, D_MODEL))]


def _token_tile(width):
    return pl.BlockSpec((TOKEN_TILE, width), lambda i: (i, 0))


def _rope_cols(t, cos, sin_signed):
    lane = lax.broadcasted_iota(jnp.int32, t.shape, 1)
    up = pltpu.roll(t, HEAD_DIM // 2, 1)
    down = pltpu.roll(t, LANES - HEAD_DIM // 2, 1)
    partner = jnp.where((lane & (HEAD_DIM // 2)) == 0, down, up)
    return t * cos + partner * sin_signed


def _ffn_proj_kernel(x_ref, g1_ref, w1_ref, w3_ref, w2_ref, gm_ref, w_ref, freq_ref,
                     x1_ref, qs_ref, ks_ref, vs_ref, qd_ref, kd_ref, vd_ref, act_ref, *, pos_blocks):
    half = HEAD_DIM // 2
    reps = LANES // half
    rows = TOKEN_TILE // reps
    first = (pl.program_id(0) % pos_blocks) * TOKEN_TILE
    grid_pos = (first + lax.broadcasted_iota(jnp.int32, (rows, LANES), 0)
                + (lax.broadcasted_iota(jnp.int32, (rows, LANES), 1) // half) * rows)
    ang = grid_pos.astype(F32) * freq_ref[...]
    cos_grid, sin_grid = jnp.cos(ang), jnp.sin(ang)
    cos_blocks, sin_blocks = [], []
    for k in range(reps):
        c = cos_grid[:, k * half:(k + 1) * half]
        s = sin_grid[:, k * half:(k + 1) * half]
        cos_blocks.append(jnp.concatenate([c] * reps, axis=1))
        sin_blocks.append(jnp.concatenate([-s, s] * (reps // 2), axis=1))
    cos = jnp.concatenate(cos_blocks, axis=0)
    sin = jnp.concatenate(sin_blocks, axis=0)

    x1 = _swiglu_half_step(x_ref[...], g1_ref[...], w1_ref, w3_ref, w2_ref, act_ref)
    x1_ref[...] = x1
    h = _rmsnorm(x1, gm_ref[...]).astype(BF16)

    def cols(lo, width):
        return jnp.dot(h, w_ref[:, lo:lo + width], preferred_element_type=F32)

    base = 3 * SB_W
    qd = cols(base, DQ_W)
    kd = cols(base + DQ_W, DKV_W)
    for c in range(DQ_W // LANES):
        qd_ref[c] = _rope_cols(qd[:, c * LANES:(c + 1) * LANES], cos, sin) * SCALE
    for c in range(DKV_W // LANES):
        kd_ref[c] = _rope_cols(kd[:, c * LANES:(c + 1) * LANES], cos, sin)
    qs_ref[...] = (cols(0, SB_W) * SCALE).astype(BF16)
    ks_ref[...] = cols(SB_W, SB_W).astype(BF16)
    vs_ref[...] = cols(2 * SB_W, SB_W).astype(BF16)
    vd = cols(base + DQ_W + DKV_W, DKV_W)
    for c in range(DKV_W // LANES):
        vd_ref[c] = vd[:, c * LANES:(c + 1) * LANES]


def _ffn_proj(x, g1, w1, w3, w2, g_mix, w_in, lane_freq, seq):
    n = x.shape[0]
    flat = ((D_MODEL, F32), (SB_W, BF16), (SB_W, BF16), (SB_W, BF16))
    slabs = (DQ_W // LANES, DKV_W // LANES, DKV_W // LANES)
    return pl.pallas_call(
        functools.partial(_ffn_proj_kernel, pos_blocks=seq // TOKEN_TILE),
        out_shape=[jax.ShapeDtypeStruct((n, w), dt) for w, dt in flat]
                  + [jax.ShapeDtypeStruct((s, n, LANES), F32) for s in slabs],
        grid=(n // TOKEN_TILE,),
        in_specs=[_token_tile(D_MODEL), _resident((1, D_MODEL))] + _ffn_weight_specs()
                 + [_resident((1, D_MODEL)), _resident_cols(D_MODEL, 0, QKV_W), _resident((1, LANES))],
        out_specs=[_token_tile(w) for w, _ in flat]
                  + [pl.BlockSpec((s, TOKEN_TILE, LANES), lambda i: (0, i, 0)) for s in slabs],
        scratch_shapes=[pltpu.VMEM((TOKEN_TILE, D_FF), BF16)],
        compiler_params=_params(("parallel",)),
        name="ffn_proj",
    )(x, g1, w1, w3, w2, g_mix, w_in, lane_freq)


def _split2(x):
    p0 = x.astype(BF16)
    p1 = (x - p0.astype(F32)).astype(BF16)
    return jnp.concatenate([p0, p1], axis=1)


def _softplus(z):
    return jnp.maximum(z, 0.0) + jnp.log(1.0 + jnp.exp2(jnp.abs(z) * -LOG2_E))


def _sb_kernel(q_ref, k_ref, v_ref, tri2_ref, tri_ref, o_ref, acc_ref, spent_ref):
    step = pl.program_id(2)
    lane = lax.broadcasted_iota(jnp.int32, (SB_BLOCK, LANES), 1)
    head_lanes = (lane < HEAD_DIM, lane >= HEAD_DIM)
    ahead = (lax.broadcasted_iota(jnp.int32, (SB_BLOCK, 2 * SB_BLOCK), 1)
             - lax.broadcasted_iota(jnp.int32, (SB_BLOCK, 2 * SB_BLOCK), 0))

    older, starts, stricts = [], [], []
    for g in range(SB_GROUP):
        blk = step * SB_GROUP + g
        first = jnp.maximum(blk - 1, 0)
        starts.append(pl.multiple_of(first * SB_BLOCK, SB_BLOCK))
        stricts.append(ahead < (blk - first) * SB_BLOCK)
        older.append(first - 1)
    pairs = [(g, hd) for g in range(SB_GROUP) for hd in range(2)]
    zs = []
    for g, hd in pairs:
        q = q_ref[g * SB_BLOCK:(g + 1) * SB_BLOCK, :]
        qh = jnp.where(head_lanes[hd], q, jnp.zeros_like(q))
        kwin = k_ref[pl.ds(starts[g], 2 * SB_BLOCK), :]
        zs.append(lax.dot_general(qh, kwin, _NT, preferred_element_type=F32))
    log_betas, pieces, least = [], [], []
    for (g, hd), z in zip(pairs, zs):
        z = jnp.where(stricts[g], z, NEG_BIG)
        cost = _softplus(z)
        log_betas.append(z - cost)
        pieces.append(_split2(cost))
        total = jnp.sum(cost, axis=1, keepdims=True)
        spent_ref[g, hd] = jnp.broadcast_to(total, (SB_BLOCK, LANES))
        least.append(jnp.min(total))
    betweens = [jnp.dot(p, tri2_ref[...], preferred_element_type=F32) for p in pieces]
    weights = [jnp.exp(lb - bt).astype(BF16) for lb, bt in zip(log_betas, betweens)]
    for (g, hd), a in zip(pairs, weights):
        vwin = v_ref[pl.ds(starts[g], 2 * SB_BLOCK), :]
        acc_ref[g, hd] = jnp.dot(a, vwin, preferred_element_type=F32)
    go = [(jnp.minimum(least[2 * g], least[2 * g + 1]) < SB_SPENT_ENOUGH).astype(jnp.int32)
          for g in range(SB_GROUP)]

    for g in range(SB_GROUP):

        def visit(state, g=g):
            kb, _ = state
            q = q_ref[g * SB_BLOCK:(g + 1) * SB_BLOCK, :]
            at = pl.multiple_of(kb * SB_BLOCK, SB_BLOCK)
            kblk = k_ref[pl.ds(at, SB_BLOCK), :]
            vblk = v_ref[pl.ds(at, SB_BLOCK), :]
            lowest = None
            for hd in range(2):
                qh = jnp.where(head_lanes[hd], q, jnp.zeros_like(q))
                z = lax.dot_general(qh, kblk, _NT, preferred_element_type=F32)
                cost = _softplus(z)
                sums = jnp.dot(_split2(cost), tri_ref[...], preferred_element_type=F32)
                spent = spent_ref[g, hd]
                a = jnp.exp(z - cost - sums[:, :SB_BLOCK] - spent)
                acc_ref[g, hd] += jnp.dot(a.astype(BF16), vblk, preferred_element_type=F32)
                spent = spent + sums[:, SB_BLOCK:]
                spent_ref[g, hd] = spent
                low = jnp.min(spent)
                lowest = low if lowest is None else jnp.minimum(lowest, low)
            return kb - 1, (lowest < SB_SPENT_ENOUGH).astype(jnp.int32)

        def live(state):
            kb, more = state
            return jnp.logical_and(kb >= 0, more > 0)

        lax.while_loop(live, visit, (older[g], go[g]))
        o_ref[g * SB_BLOCK:(g + 1) * SB_BLOCK, :] = jnp.where(
            head_lanes[0], acc_ref[g, 0], acc_ref[g, 1]).astype(o_ref.dtype)


def _suffix_matrix(n):
    j = lax.broadcasted_iota(jnp.int32, (n, n), 0)
    s = lax.broadcasted_iota(jnp.int32, (n, n), 1)
    return (j > s).astype(BF16)


def _sb_attention(q, k, v, batch, seq):
    q3, k3, v3 = (t.reshape(batch, seq, SB_W) for t in (q, k, v))
    tri2 = jnp.tile(_suffix_matrix(2 * SB_BLOCK), (2, 1))
    tri = jnp.tile(jnp.concatenate([_suffix_matrix(SB_BLOCK), jnp.ones((SB_BLOCK, LANES), BF16)], axis=1), (2, 1))
    rows = SB_GROUP * SB_BLOCK
    qblk = pl.BlockSpec((None, rows, LANES), lambda b, p, i: (b, i, p))
    whole = pl.BlockSpec((None, seq, LANES), lambda b, p, i: (b, 0, p))
    scratch = pltpu.VMEM((SB_GROUP, 2, SB_BLOCK, LANES), F32)
    out = pl.pallas_call(
        _sb_kernel,
        out_shape=jax.ShapeDtypeStruct((batch, seq, SB_W), BF16),
        grid=(batch, SB_W // LANES, seq // rows),
        in_specs=[qblk, whole, whole, _resident(tri2.shape), _resident(tri.shape)],
        out_specs=qblk,
        scratch_shapes=[scratch, scratch],
        compiler_params=_params(("parallel", "parallel", "arbitrary")),
        name="sb_attn",
    )(q3, k3, v3, tri2, tri)
    return out.reshape(batch * seq, SB_W)


def _strided(start, size, stride):
    return pl.ds(start, size) if stride == 1 else pl.ds(start, size, stride=stride)


def _band_attention(blocks, head_lanes):
    scores = []
    for q, kcat, _, _ in blocks:
        for hd in range(2):
            qh = jnp.where(head_lanes[hd], q, jnp.zeros_like(q))
            scores.append(lax.dot_general(qh, kcat, _NT, preferred_element_type=F32))
    probs, stats = [], []
    for i, s in enumerate(scores):
        s = jnp.where(blocks[i // 2][3], s, NEG_BIG)
        mx = jnp.max(s, axis=-1, keepdims=True)
        e = jnp.exp(s - mx)
        probs.append(e.astype(BF16))
        stats.append((mx, jnp.sum(e, axis=-1, keepdims=True)))
    results = []
    for b, (_, _, vcat, _) in enumerate(blocks):
        outs, lses = [], []
        for hd in range(2):
            mx, den = stats[2 * b + hd]
            o = jnp.dot(probs[2 * b + hd], vcat, preferred_element_type=F32)
            outs.append(o / den)
            lses.append(jnp.broadcast_to(mx + jnp.log(den), (DIL_BAND, LANES)))
        results.append((jnp.where(head_lanes[0], outs[0], outs[1]),
                        jnp.where(head_lanes[0], lses[0], lses[1])))
    return results


def _dil_kernel(q0_ref, q1_ref, q2_ref, kprev_ref, k_ref, vprev_ref, v_ref, o_ref, obuf, lbuf, staged):
    step = pl.program_id(2)
    row = lax.broadcasted_iota(jnp.int32, (DIL_BAND, 2 * DIL_BAND), 0)
    col = lax.broadcasted_iota(jnp.int32, (DIL_BAND, 2 * DIL_BAND), 1)
    band = jnp.logical_and(col >= row, col <= row + DIL_BAND)
    band_first = jnp.logical_and(band, jnp.logical_or(col >= DIL_BAND, step > 0))
    lane = lax.broadcasted_iota(jnp.int32, (DIL_BAND, LANES), 1)
    head_lanes = (lane < HEAD_DIM, lane >= HEAD_DIM)
    q_refs = (q0_ref, q1_ref, q2_ref)
    part = DIL_WINDOW // DIL_SPLIT

    def operands(g):
        return q_refs[g], kprev_ref, k_ref, vprev_ref, v_ref

    def fetch(g, dil, which, r, m0, n):
        if isinstance(r, tuple):
            lo, hi = r
            inner = dil // DIL_SPLIT
            return staged[which, _strided(lo * part + hi + inner * m0, n, inner), :]
        return operands(g)[which][_strided(r + dil * m0, n, dil), :]

    def load(g, dil, r, j):
        m0 = DIL_BAND * j
        if dil == 1 and not isinstance(j, int):
            m0 = pl.multiple_of(m0, DIL_BAND)
        token_r = r[0] + DIL_SPLIT * r[1] if isinstance(r, tuple) else r
        q = fetch(g, dil, 0, r, m0, DIL_BAND).astype(BF16)
        if isinstance(j, int) and j == 0:
            tail = DIL_WINDOW // dil - DIL_BAND
            kcat = jnp.concatenate([fetch(g, dil, 1, r, tail, DIL_BAND), fetch(g, dil, 2, r, 0, DIL_BAND)], axis=0)
            vcat = jnp.concatenate([fetch(g, dil, 3, r, tail, DIL_BAND), fetch(g, dil, 4, r, 0, DIL_BAND)], axis=0)
            valid = band_first
        else:
            kcat = fetch(g, dil, 2, r, m0 - DIL_BAND, 2 * DIL_BAND)
            vcat = fetch(g, dil, 4, r, m0 - DIL_BAND, 2 * DIL_BAND)
            valid = band
        return token_r + dil * m0, (q, kcat.astype(BF16), vcat.astype(BF16), valid)

    def run(g, dil, where):
        loaded = [load(g, dil, r, j) for r, j in where]
        results = _band_attention([blk for _, blk in loaded], head_lanes)
        for (base, _), (out, lse) in zip(loaded, results):
            obuf[g, _strided(base, DIL_BAND, dil), :] = out
            lbuf[g, _strided(base, DIL_BAND, dil), :] = lse

    for g, (_, dil) in enumerate(DIL_GROUPS):
        per_residue = DIL_WINDOW // (dil * DIL_BAND)
        if dil == 1:
            run(g, dil, [(0, j) for j in range(DIL_BATCH)])

            def body(i, _, g=g, dil=dil):
                run(g, dil, [(0, i * DIL_BATCH + j) for j in range(DIL_BATCH)])
                return 0

            lax.fori_loop(1, per_residue // DIL_BATCH, body, 0)
        elif dil <= DIL_SPLIT:
            def body(r, _, g=g, dil=dil, per_residue=per_residue):
                for j0 in range(0, per_residue, DIL_BATCH):
                    run(g, dil, [(r, j) for j in range(j0, min(j0 + DIL_BATCH, per_residue))])
                return 0

            lax.fori_loop(0, dil, body, 0)
        else:
            for which, ref in enumerate(operands(g)):
                for lo in range(DIL_SPLIT):
                    staged[which, lo * part:(lo + 1) * part, :] = ref[pl.ds(lo, part, stride=DIL_SPLIT), :]

            def body(hi, _, g=g, dil=dil, per_residue=per_residue):
                run(g, dil, [((lo, hi), j) for lo in range(DIL_SPLIT) for j in range(per_residue)])
                return 0

            lax.fori_loop(0, dil // DIL_SPLIT, body, 0)

    chunk = 2 * DIL_BAND
    for c in range(DIL_WINDOW // chunk):
        rows = slice(c * chunk, (c + 1) * chunk)
        lses = [lbuf[g, rows, :] for g in range(len(DIL_GROUPS))]
        top = functools.reduce(jnp.maximum, lses)
        ws = [jnp.exp(l - top) for l in lses]
        num = sum(w * obuf[g, rows, :] for g, w in enumerate(ws))
        o_ref[rows, :] = (num / sum(ws)).astype(o_ref.dtype)


def _dil_attention(q, k, v, batch, seq):
    n_groups = len(DIL_GROUPS)
    pairs = DKV_W // LANES
    q3 = q.reshape(n_groups * pairs, batch, seq, LANES)
    k3 = k.reshape(pairs, batch, seq, LANES)
    v3 = v.reshape(pairs, batch, seq, LANES)
    window = (None, None, DIL_WINDOW, LANES)

    def q_spec(g):
        return pl.BlockSpec(window, lambda b, p, i: (g * pairs + p, b, i, 0))

    cur = pl.BlockSpec(window, lambda b, p, i: (p, b, i, 0))
    prev = pl.BlockSpec(window, lambda b, p, i: (p, b, jnp.maximum(i - 1, 0), 0))
    out_spec = pl.BlockSpec((None, DIL_WINDOW, LANES), lambda b, p, i: (b, i, p))
    buf = pltpu.VMEM((n_groups, DIL_WINDOW, LANES), F32)
    staged = pltpu.VMEM((5, DIL_WINDOW, LANES), F32)
    out = pl.pallas_call(
        _dil_kernel,
        out_shape=jax.ShapeDtypeStruct((batch, seq, DKV_W), BF16),
        grid=(batch, pairs, seq // DIL_WINDOW),
        in_specs=[q_spec(g) for g in range(n_groups)] + [prev, cur, prev, cur],
        out_specs=out_spec,
        scratch_shapes=[buf, buf, staged],
        compiler_params=_params(("parallel", "parallel", "arbitrary")),
        name="dil_attn",
    )(q3, q3, q3, k3, k3, v3, v3)
    return out.reshape(batch * seq, DKV_W)


def _mix_ffn_kernel(*refs, final_norm):
    (x_ref, gm_ref, wg_ref, bg_ref, osb_ref, odl_ref, wsb_ref, wdl_ref, wo_ref,
     g2_ref, w1_ref, w3_ref, w2_ref) = refs[:13]
    gf_ref = refs[13] if final_norm else None
    o_ref, act_ref = refs[-2:]
    x = x_ref[...]
    h = _rmsnorm(x, gm_ref[...]).astype(BF16)

    def gate(lo):
        pre = jnp.dot(h, wg_ref[:, lo:lo + D_MODEL], preferred_element_type=F32)
        return jax.nn.sigmoid(pre + bg_ref[:, lo:lo + D_MODEL])

    mixed = gate(0) * jnp.dot(osb_ref[...], wsb_ref[...], preferred_element_type=F32)
    mixed = mixed + gate(D_MODEL) * jnp.dot(odl_ref[...], wdl_ref[...], preferred_element_type=F32)
    x2 = x + jnp.dot(mixed.astype(BF16), wo_ref[...], preferred_element_type=F32)
    y = _swiglu_half_step(x2, g2_ref[...], w1_ref, w3_ref, w2_ref, act_ref)
    if final_norm:
        y = _rmsnorm(y, gf_ref[...])
    o_ref[...] = y


def _mix_ffn(x, g_mix, w_in, b_gate, o_sb, o_dl, w_sb_out, w_dil_out, w_o, g2, w1, w3, w2, g_final=None):
    n = x.shape[0]
    in_specs = ([_token_tile(D_MODEL), _resident((1, D_MODEL)), _resident_cols(D_MODEL, QKV_W, 2 * D_MODEL),
                 _resident((1, 2 * D_MODEL)), _token_tile(SB_W), _token_tile(DKV_W),
                 _resident((SB_W, D_MODEL)), _resident((DKV_W, D_MODEL)), _resident((D_MODEL, D_MODEL)),
                 _resident((1, D_MODEL))] + _ffn_weight_specs())
    args = [x, g_mix, w_in, b_gate, o_sb, o_dl, w_sb_out, w_dil_out, w_o, g2, w1, w3, w2]
    if g_final is not None:
        in_specs.append(_resident((1, D_MODEL)))
        args.append(g_final)
    return pl.pallas_call(
        functools.partial(_mix_ffn_kernel, final_norm=g_final is not None),
        out_shape=jax.ShapeDtypeStruct((n, D_MODEL), F32),
        grid=(n // TOKEN_TILE,),
        in_specs=in_specs,
        out_specs=_token_tile(D_MODEL),
        scratch_shapes=[pltpu.VMEM((TOKEN_TILE, D_FF), BF16)],
        compiler_params=_params(("parallel",)),
        name="mix_ffn",
    )(*args)


def _rope_lane_freq():
    half = HEAD_DIM // 2
    inv_freq = ROPE_THETA ** (-jnp.arange(half, dtype=F32) / half)
    return jnp.tile(inv_freq, LANES // half).reshape(1, LANES)


def kernel(x, g_ffn1, w1_a, w3_a, w2_a, g_mix, w_in, b_gate, w_sb_out, w_dil_out, w_o,
           g_ffn2, w1_b, w3_b, w2_b, g_final):
    batch, seq, _ = x.shape
    depth = g_ffn1.shape[0]
    assert seq % DIL_WINDOW == 0 and seq % (SB_GROUP * SB_BLOCK) == 0 and seq % TOKEN_TILE == 0
    assert all(w == DIL_BAND * d and DIL_WINDOW % (DIL_BAND * d) == 0 for w, d in DIL_GROUPS)
    assert all(d <= DIL_SPLIT or (d % DIL_SPLIT == 0 and d // DIL_SPLIT <= DIL_SPLIT) for _, d in DIL_GROUPS)
    lane_freq = _rope_lane_freq()
    row = lambda t: t.reshape(1, -1)
    bf = lambda t: t.astype(BF16)
    xt = x.reshape(batch * seq, D_MODEL)
    for l in range(depth):
        w_in_l = bf(w_in[l])
        xt, q_sb, k_sb, v_sb, q_dl, k_dl, v_dl = _ffn_proj(
            xt, row(g_ffn1[l]), bf(w1_a[l]), bf(w3_a[l]), bf(w2_a[l]),
            row(g_mix[l]), w_in_l, lane_freq, seq)
        o_sb = _sb_attention(q_sb, k_sb, v_sb, batch, seq)
        o_dl = _dil_attention(q_dl, k_dl, v_dl, batch, seq)
        xt = _mix_ffn(xt, row(g_mix[l]), w_in_l, row(b_gate[l]), o_sb, o_dl,
                      bf(w_sb_out[l]), bf(w_dil_out[l]), bf(w_o[l]),
                      row(g_ffn2[l]), bf(w1_b[l]), bf(w3_b[l]), bf(w2_b[l]),
                      row(g_final) if l == depth - 1 else None)
    return xt.reshape(batch, seq, D_MODEL)
```

```python
import functools

import jax
import jax.numpy as jnp
from jax import lax
from jax.experimental import pallas as pl
from jax.experimental.pallas import tpu as pltpu

D_MODEL = 1024
HEAD_DIM = 64
SB_HEADS = 8
DIL_GROUPS = ((128, 1), (512, 4), (2048, 16))
DIL_KV_HEADS = 4
D_FF = 2816
ROPE_THETA = 10000.0
RMS_EPS = 1e-6

SB_W = SB_HEADS * HEAD_DIM
DKV_W = DIL_KV_HEADS * HEAD_DIM
DQ_W = DKV_W * len(DIL_GROUPS)
QKV_W = 3 * SB_W + DQ_W + 2 * DKV_W
SCALE = HEAD_DIM ** -0.5

LANES = 128
TOKEN_TILE = 512
FF_CHUNK = 256
SB_BLOCK = 128
SB_GROUP = 16
DIL_BAND = 128
DIL_WINDOW = DIL_BAND * max(d for _, d in DIL_GROUPS)
DIL_BATCH = 4
DIL_SPLIT = 4
SB_SPENT_ENOUGH = 88.0
NEG_BIG = -1e30
LOG2_E = 1.4426950408889634
VMEM_LIMIT = 56 * 1024 * 1024

F32 = jnp.float32
BF16 = jnp.bfloat16
_NT = (((1,), (1,)), ((), ()))


def _rmsnorm(x, g):
    ms = jnp.mean(x * x, axis=-1, keepdims=True)
    return x * lax.rsqrt(ms + RMS_EPS) * g


def _resident(shape):
    return pl.BlockSpec(shape, lambda *_: (0,) * len(shape), pipeline_mode=pl.Buffered(1))


def _resident_cols(rows, lo, width):
    return pl.BlockSpec((pl.Element(rows), pl.Element(width)), lambda *_: (0, lo),
                        pipeline_mode=pl.Buffered(1))


def _params(semantics):
    return pltpu.CompilerParams(dimension_semantics=semantics, vmem_limit_bytes=VMEM_LIMIT)


def _swiglu_half_step(x, g, w1_ref, w3_ref, w2_ref, act_ref):
    h = _rmsnorm(x, g).astype(BF16)
    for lo in range(0, D_FF, FF_CHUNK):
        sl = slice(lo, lo + FF_CHUNK)
        u = jnp.dot(h, w1_ref[:, sl], preferred_element_type=F32)
        v = jnp.dot(h, w3_ref[:, sl], preferred_element_type=F32)
        act_ref[:, sl] = (u * jax.nn.sigmoid(u) * v).astype(BF16)
    return x + 0.5 * jnp.dot(act_ref[...], w2_ref[...], preferred_element_type=F32)


def _ffn_weight_specs():
    return [_resident((D_MODEL, D_FF)), _resident((D_MODEL, D_FF)), _resident((D_FF, D_MODEL))]


def _token_tile(width):
    return pl.BlockSpec((TOKEN_TILE, width), lambda i: (i, 0))


def _rope_cols(t, cos, sin_signed):
    lane = lax.broadcasted_iota(jnp.int32, t.shape, 1)
    up = pltpu.roll(t, HEAD_DIM // 2, 1)
    down = pltpu.roll(t, LANES - HEAD_DIM // 2, 1)
    partner = jnp.where((lane & (HEAD_DIM // 2)) == 0, down, up)
    return t * cos + partner * sin_signed


def _ffn_proj_kernel(x_ref, g1_ref, w1_ref, w3_ref, w2_ref, gm_ref, w_ref, freq_ref,
                     x1_ref, qs_ref, ks_ref, vs_ref, qd_ref, kd_ref, vd_ref, act_ref, *, pos_blocks):
    half = HEAD_DIM // 2
    reps = LANES // half
    rows = TOKEN_TILE // reps
    first = (pl.program_id(0) % pos_blocks) * TOKEN_TILE
    grid_pos = (first + lax.broadcasted_iota(jnp.int32, (rows, LANES), 0)
                + (lax.broadcasted_iota(jnp.int32, (rows, LANES), 1) // half) * rows)
    ang = grid_pos.astype(F32) * freq_ref[...]
    cos_grid, sin_grid = jnp.cos(ang), jnp.sin(ang)
    cos_blocks, sin_blocks = [], []
    for k in range(reps):
        c = cos_grid[:, k * half:(k + 1) * half]
        s = sin_grid[:, k * half:(k + 1) * half]
        cos_blocks.append(jnp.concatenate([c] * reps, axis=1))
        sin_blocks.append(jnp.concatenate([-s, s] * (reps // 2), axis=1))
    cos = jnp.concatenate(cos_blocks, axis=0)
    sin = jnp.concatenate(sin_blocks, axis=0)

    x1 = _swiglu_half_step(x_ref[...], g1_ref[...], w1_ref, w3_ref, w2_ref, act_ref)
    x1_ref[...] = x1
    h = _rmsnorm(x1, gm_ref[...]).astype(BF16)

    def cols(lo, width):
        return jnp.dot(h, w_ref[:, lo:lo + width], preferred_element_type=F32)

    base = 3 * SB_W
    qd = cols(base, DQ_W)
    kd = cols(base + DQ_W, DKV_W)
    for c in range(DQ_W // LANES):
        sl = slice(c * LANES, (c + 1) * LANES)
        qd_ref[:, sl] = _rope_cols(qd[:, sl], cos, sin) * SCALE
    for c in range(DKV_W // LANES):
        sl = slice(c * LANES, (c + 1) * LANES)
        kd_ref[:, sl] = _rope_cols(kd[:, sl], cos, sin)
    qs_ref[...] = (cols(0, SB_W) * SCALE).astype(BF16)
    ks_ref[...] = cols(SB_W, SB_W).astype(BF16)
    vs_ref[...] = cols(2 * SB_W, SB_W).astype(BF16)
    vd_ref[...] = cols(base + DQ_W + DKV_W, DKV_W)


def _ffn_proj(x, g1, w1, w3, w2, g_mix, w_in, lane_freq, seq):
    n = x.shape[0]
    outs = ((D_MODEL, F32), (SB_W, BF16), (SB_W, BF16), (SB_W, BF16), (DQ_W, F32), (DKV_W, F32), (DKV_W, F32))
    return pl.pallas_call(
        functools.partial(_ffn_proj_kernel, pos_blocks=seq // TOKEN_TILE),
        out_shape=[jax.ShapeDtypeStruct((n, w), dt) for w, dt in outs],
        grid=(n // TOKEN_TILE,),
        in_specs=[_token_tile(D_MODEL), _resident((1, D_MODEL))] + _ffn_weight_specs()
                 + [_resident((1, D_MODEL)), _resident_cols(D_MODEL, 0, QKV_W), _resident((1, LANES))],
        out_specs=[_token_tile(w) for w, _ in outs],
        scratch_shapes=[pltpu.VMEM((TOKEN_TILE, D_FF), BF16)],
        compiler_params=_params(("parallel",)),
        name="ffn_proj",
    )(x, g1, w1, w3, w2, g_mix, w_in, lane_freq)


def _split2(x):
    p0 = x.astype(BF16)
    p1 = (x - p0.astype(F32)).astype(BF16)
    return jnp.concatenate([p0, p1], axis=1)


def _softplus(z):
    return jnp.maximum(z, 0.0) + jnp.log(1.0 + jnp.exp2(jnp.abs(z) * -LOG2_E))


def _sb_kernel(q_ref, k_ref, v_ref, tri2_ref, tri_ref, o_ref, acc_ref, spent_ref):
    step = pl.program_id(2)
    lane = lax.broadcasted_iota(jnp.int32, (SB_BLOCK, LANES), 1)
    head_lanes = (lane < HEAD_DIM, lane >= HEAD_DIM)
    ahead = (lax.broadcasted_iota(jnp.int32, (SB_BLOCK, 2 * SB_BLOCK), 1)
             - lax.broadcasted_iota(jnp.int32, (SB_BLOCK, 2 * SB_BLOCK), 0))

    older, starts = [], []
    for g in range(SB_GROUP):
        first = jnp.maximum(step * SB_GROUP + g - 1, 0)
        starts.append(pl.multiple_of(first * SB_BLOCK, SB_BLOCK))
        older.append(first - 1)
    first_strict = ahead < jnp.minimum(step, 1) * SB_BLOCK
    own_strict = ahead[:, SB_BLOCK:] < SB_BLOCK

    def causal(g, z):
        if g == 0:
            return jnp.where(first_strict, z, NEG_BIG)
        return jnp.concatenate([z[:, :SB_BLOCK], jnp.where(own_strict, z[:, SB_BLOCK:], NEG_BIG)], axis=1)

    pairs = [(g, hd) for g in range(SB_GROUP) for hd in range(2)]
    zs = []
    for g, hd in pairs:
        q = q_ref[g * SB_BLOCK:(g + 1) * SB_BLOCK, :]
        qh = jnp.where(head_lanes[hd], q, jnp.zeros_like(q))
        kwin = k_ref[pl.ds(starts[g], 2 * SB_BLOCK), :]
        zs.append(lax.dot_general(qh, kwin, _NT, preferred_element_type=F32))
    log_betas, pieces, least = [], [], []
    for (g, hd), z in zip(pairs, zs):
        z = causal(g, z)
        cost = _softplus(z)
        log_betas.append(z - cost)
        pieces.append(_split2(cost))
        total = jnp.sum(cost, axis=1, keepdims=True)
        spent_ref[g, hd] = jnp.broadcast_to(total, (SB_BLOCK, LANES))
        least.append(jnp.min(total))
    betweens = [jnp.dot(p, tri2_ref[...], preferred_element_type=F32) for p in pieces]
    weights = [jnp.exp(lb - bt).astype(BF16) for lb, bt in zip(log_betas, betweens)]
    for (g, hd), a in zip(pairs, weights):
        vwin = v_ref[pl.ds(starts[g], 2 * SB_BLOCK), :]
        acc_ref[g, hd] = jnp.dot(a, vwin, preferred_element_type=F32)
    go = [(jnp.minimum(least[2 * g], least[2 * g + 1]) < SB_SPENT_ENOUGH).astype(jnp.int32)
          for g in range(SB_GROUP)]

    for g in range(SB_GROUP):

        def visit(state, g=g):
            kb, _ = state
            q = q_ref[g * SB_BLOCK:(g + 1) * SB_BLOCK, :]
            at = pl.multiple_of(kb * SB_BLOCK, SB_BLOCK)
            kblk = k_ref[pl.ds(at, SB_BLOCK), :]
            vblk = v_ref[pl.ds(at, SB_BLOCK), :]
            lowest = None
            for hd in range(2):
                qh = jnp.where(head_lanes[hd], q, jnp.zeros_like(q))
                z = lax.dot_general(qh, kblk, _NT, preferred_element_type=F32)
                cost = _softplus(z)
                sums = jnp.dot(_split2(cost), tri_ref[...], preferred_element_type=F32)
                spent = spent_ref[g, hd]
                a = jnp.exp(z - cost - sums[:, :SB_BLOCK] - spent)
                acc_ref[g, hd] += jnp.dot(a.astype(BF16), vblk, preferred_element_type=F32)
                spent = spent + sums[:, SB_BLOCK:]
                spent_ref[g, hd] = spent
                low = jnp.min(spent)
                lowest = low if lowest is None else jnp.minimum(lowest, low)
            return kb - 1, (lowest < SB_SPENT_ENOUGH).astype(jnp.int32)

        def live(state):
            kb, more = state
            return jnp.logical_and(kb >= 0, more > 0)

        lax.while_loop(live, visit, (older[g], go[g]))
        o_ref[g * SB_BLOCK:(g + 1) * SB_BLOCK, :] = jnp.where(
            head_lanes[0], acc_ref[g, 0], acc_ref[g, 1]).astype(o_ref.dtype)


def _suffix_matrix(n):
    j = lax.broadcasted_iota(jnp.int32, (n, n), 0)
    s = lax.broadcasted_iota(jnp.int32, (n, n), 1)
    return (j > s).astype(BF16)


def _sb_attention(q, k, v, batch, seq):
    q3, k3, v3 = (t.reshape(batch, seq, SB_W) for t in (q, k, v))
    tri2 = jnp.tile(_suffix_matrix(2 * SB_BLOCK), (2, 1))
    tri = jnp.tile(jnp.concatenate([_suffix_matrix(SB_BLOCK), jnp.ones((SB_BLOCK, LANES), BF16)], axis=1), (2, 1))
    rows = SB_GROUP * SB_BLOCK
    qblk = pl.BlockSpec((None, rows, LANES), lambda b, p, i: (b, i, p))
    whole = pl.BlockSpec((None, seq, LANES), lambda b, p, i: (b, 0, p))
    scratch = pltpu.VMEM((SB_GROUP, 2, SB_BLOCK, LANES), F32)
    out = pl.pallas_call(
        _sb_kernel,
        out_shape=jax.ShapeDtypeStruct((batch, seq, SB_W), BF16),
        grid=(batch, SB_W // LANES, seq // rows),
        in_specs=[qblk, whole, whole, _resident(tri2.shape), _resident(tri.shape)],
        out_specs=qblk,
        scratch_shapes=[scratch, scratch],
        compiler_params=_params(("parallel", "parallel", "arbitrary")),
        name="sb_attn",
    )(q3, k3, v3, tri2, tri)
    return out.reshape(batch * seq, SB_W)


def _strided(start, size, stride):
    return pl.ds(start, size) if stride == 1 else pl.ds(start, size, stride=stride)


def _band_attention(blocks, head_lanes):
    scores = []
    for q, kcat, _, _ in blocks:
        for hd in range(2):
            qh = jnp.where(head_lanes[hd], q, jnp.zeros_like(q))
            scores.append(lax.dot_general(qh, kcat, _NT, preferred_element_type=F32))
    probs, stats = [], []
    for i, s in enumerate(scores):
        s = jnp.where(blocks[i // 2][3], s, NEG_BIG)
        mx = jnp.max(s, axis=-1, keepdims=True)
        e = jnp.exp(s - mx)
        probs.append(e.astype(BF16))
        stats.append((mx, jnp.sum(e, axis=-1, keepdims=True)))
    results = []
    for b, (_, _, vcat, _) in enumerate(blocks):
        outs, lses = [], []
        for hd in range(2):
            mx, den = stats[2 * b + hd]
            o = jnp.dot(probs[2 * b + hd], vcat, preferred_element_type=F32)
            outs.append(o / den)
            lses.append(jnp.broadcast_to(mx + jnp.log(den), (DIL_BAND, LANES)))
        results.append((jnp.where(head_lanes[0], outs[0], outs[1]),
                        jnp.where(head_lanes[0], lses[0], lses[1])))
    return results


def _dil_kernel(q0_ref, q1_ref, q2_ref, kprev_ref, k_ref, vprev_ref, v_ref, o_ref, obuf, lbuf, staged):
    step = pl.program_id(2)
    row = lax.broadcasted_iota(jnp.int32, (DIL_BAND, 2 * DIL_BAND), 0)
    col = lax.broadcasted_iota(jnp.int32, (DIL_BAND, 2 * DIL_BAND), 1)
    band = jnp.logical_and(col >= row, col <= row + DIL_BAND)
    band_first = jnp.logical_and(band, jnp.logical_or(col >= DIL_BAND, step > 0))
    lane = lax.broadcasted_iota(jnp.int32, (DIL_BAND, LANES), 1)
    head_lanes = (lane < HEAD_DIM, lane >= HEAD_DIM)
    q_refs = (q0_ref, q1_ref, q2_ref)
    part = DIL_WINDOW // DIL_SPLIT

    def operands(g):
        return q_refs[g], kprev_ref, k_ref, vprev_ref, v_ref

    def fetch(g, dil, which, r, m0, n):
        if isinstance(r, tuple):
            lo, hi = r
            inner = dil // DIL_SPLIT
            return staged[which, _strided(lo * part + hi + inner * m0, n, inner), :]
        return operands(g)[which][_strided(r + dil * m0, n, dil), :]

    def load(g, dil, r, j):
        m0 = DIL_BAND * j
        if dil == 1 and not isinstance(j, int):
            m0 = pl.multiple_of(m0, DIL_BAND)
        token_r = r[0] + DIL_SPLIT * r[1] if isinstance(r, tuple) else r
        q = fetch(g, dil, 0, r, m0, DIL_BAND).astype(BF16)
        if isinstance(j, int) and j == 0:
            tail = DIL_WINDOW // dil - DIL_BAND
            kcat = jnp.concatenate([fetch(g, dil, 1, r, tail, DIL_BAND), fetch(g, dil, 2, r, 0, DIL_BAND)], axis=0)
            vcat = jnp.concatenate([fetch(g, dil, 3, r, tail, DIL_BAND), fetch(g, dil, 4, r, 0, DIL_BAND)], axis=0)
            valid = band_first
        else:
            kcat = fetch(g, dil, 2, r, m0 - DIL_BAND, 2 * DIL_BAND)
            vcat = fetch(g, dil, 4, r, m0 - DIL_BAND, 2 * DIL_BAND)
            valid = band
        return token_r + dil * m0, (q, kcat.astype(BF16), vcat.astype(BF16), valid)

    def run(g, dil, where):
        loaded = [load(g, dil, r, j) for r, j in where]
        results = _band_attention([blk for _, blk in loaded], head_lanes)
        for (base, _), (out, lse) in zip(loaded, results):
            obuf[g, _strided(base, DIL_BAND, dil), :] = out
            lbuf[g, _strided(base, DIL_BAND, dil), :] = lse

    for g, (_, dil) in enumerate(DIL_GROUPS):
        per_residue = DIL_WINDOW // (dil * DIL_BAND)
        if dil == 1:
            run(g, dil, [(0, j) for j in range(DIL_BATCH)])

            def body(i, _, g=g, dil=dil):
                run(g, dil, [(0, i * DIL_BATCH + j) for j in range(DIL_BATCH)])
                return 0

            lax.fori_loop(1, per_residue // DIL_BATCH, body, 0)
        elif dil <= DIL_SPLIT:
            def body(r, _, g=g, dil=dil, per_residue=per_residue):
                for j0 in range(0, per_residue, DIL_BATCH):
                    run(g, dil, [(r, j) for j in range(j0, min(j0 + DIL_BATCH, per_residue))])
                return 0

            lax.fori_loop(0, dil, body, 0)
        else:
            for which, ref in enumerate(operands(g)):
                for lo in range(DIL_SPLIT):
                    staged[which, lo * part:(lo + 1) * part, :] = ref[pl.ds(lo, part, stride=DIL_SPLIT), :]

            def body(hi, _, g=g, dil=dil, per_residue=per_residue):
                run(g, dil, [((lo, hi), j) for lo in range(DIL_SPLIT) for j in range(per_residue)])
                return 0

            lax.fori_loop(0, dil // DIL_SPLIT, body, 0)

    chunk = 2 * DIL_BAND
    for c in range(DIL_WINDOW // chunk):
        rows = slice(c * chunk, (c + 1) * chunk)
        lses = [lbuf[g, rows, :] for g in range(len(DIL_GROUPS))]
        top = functools.reduce(jnp.maximum, lses)
        ws = [jnp.exp(l - top) for l in lses]
        num = sum(w * obuf[g, rows, :] for g, w in enumerate(ws))
        o_ref[rows, :] = (num / sum(ws)).astype(o_ref.dtype)


def _dil_attention(q, k, v, batch, seq):
    n_groups = len(DIL_GROUPS)
    pairs = DKV_W // LANES
    q3 = q.reshape(batch, seq, DQ_W)
    k3 = k.reshape(batch, seq, DKV_W)
    v3 = v.reshape(batch, seq, DKV_W)

    def q_spec(g):
        return pl.BlockSpec((None, DIL_WINDOW, LANES), lambda b, p, i: (b, i, g * pairs + p))

    cur = pl.BlockSpec((None, DIL_WINDOW, LANES), lambda b, p, i: (b, i, p))
    prev = pl.BlockSpec((None, DIL_WINDOW, LANES), lambda b, p, i: (b, jnp.maximum(i - 1, 0), p))
    buf = pltpu.VMEM((n_groups, DIL_WINDOW, LANES), F32)
    staged = pltpu.VMEM((5, DIL_WINDOW, LANES), F32)
    out = pl.pallas_call(
        _dil_kernel,
        out_shape=jax.ShapeDtypeStruct((batch, seq, DKV_W), BF16),
        grid=(batch, pairs, seq // DIL_WINDOW),
        in_specs=[q_spec(g) for g in range(n_groups)] + [prev, cur, prev, cur],
        out_specs=cur,
        scratch_shapes=[buf, buf, staged],
        compiler_params=_params(("parallel", "parallel", "arbitrary")),
        name="dil_attn",
    )(q3, q3, q3, k3, k3, v3, v3)
    return out.reshape(batch * seq, DKV_W)


def _mix_ffn_kernel(*refs, final_norm):
    (x_ref, gm_ref, wg_ref, bg_ref, osb_ref, odl_ref, wsb_ref, wdl_ref, wo_ref,
     g2_ref, w1_ref, w3_ref, w2_ref) = refs[:13]
    gf_ref = refs[13] if final_norm else None
    o_ref, act_ref = refs[-2:]
    x = x_ref[...]
    h = _rmsnorm(x, gm_ref[...]).astype(BF16)

    def gate(lo):
        pre = jnp.dot(h, wg_ref[:, lo:lo + D_MODEL], preferred_element_type=F32)
        return jax.nn.sigmoid(pre + bg_ref[:, lo:lo + D_MODEL])

    mixed = gate(0) * jnp.dot(osb_ref[...], wsb_ref[...], preferred_element_type=F32)
    mixed = mixed + gate(D_MODEL) * jnp.dot(odl_ref[...], wdl_ref[...], preferred_element_type=F32)
    x2 = x + jnp.dot(mixed.astype(BF16), wo_ref[...], preferred_element_type=F32)
    y = _swiglu_half_step(x2, g2_ref[...], w1_ref, w3_ref, w2_ref, act_ref)
    if final_norm:
        y = _rmsnorm(y, gf_ref[...])
    o_ref[...] = y


def _mix_ffn(x, g_mix, w_in, b_gate, o_sb, o_dl, w_sb_out, w_dil_out, w_o, g2, w1, w3, w2, g_final=None):
    n = x.shape[0]
    in_specs = ([_token_tile(D_MODEL), _resident((1, D_MODEL)), _resident_cols(D_MODEL, QKV_W, 2 * D_MODEL),
                 _resident((1, 2 * D_MODEL)), _token_tile(SB_W), _token_tile(DKV_W),
                 _resident((SB_W, D_MODEL)), _resident((DKV_W, D_MODEL)), _resident((D_MODEL, D_MODEL)),
                 _resident((1, D_MODEL))] + _ffn_weight_specs())
    args = [x, g_mix, w_in, b_gate, o_sb, o_dl, w_sb_out, w_dil_out, w_o, g2, w1, w3, w2]
    if g_final is not None:
        in_specs.append(_resident((1, D_MODEL)))
        args.append(g_final)
    return pl.pallas_call(
        functools.partial(_mix_ffn_kernel, final_norm=g_final is not None),
        out_shape=jax.ShapeDtypeStruct((n, D_MODEL), F32),
        grid=(n // TOKEN_TILE,),
        in_specs=in_specs,
        out_specs=_token_tile(D_MODEL),
        scratch_shapes=[pltpu.VMEM((TOKEN_TILE, D_FF), BF16)],
        compiler_params=_params(("parallel",)),
        name="mix_ffn",
    )(*args)


def _rope_lane_freq():
    half = HEAD_DIM // 2
    inv_freq = ROPE_THETA ** (-jnp.arange(half, dtype=F32) / half)
    return jnp.tile(inv_freq, LANES // half).reshape(1, LANES)


def kernel(x, g_ffn1, w1_a, w3_a, w2_a, g_mix, w_in, b_gate, w_sb_out, w_dil_out, w_o,
           g_ffn2, w1_b, w3_b, w2_b, g_final):
    batch, seq, _ = x.shape
    depth = g_ffn1.shape[0]
    assert seq % DIL_WINDOW == 0 and seq % (SB_GROUP * SB_BLOCK) == 0 and seq % TOKEN_TILE == 0
    assert all(w == DIL_BAND * d and DIL_WINDOW % (DIL_BAND * d) == 0 for w, d in DIL_GROUPS)
    assert all(d <= DIL_SPLIT or (d % DIL_SPLIT == 0 and d // DIL_SPLIT <= DIL_SPLIT) for _, d in DIL_GROUPS)
    lane_freq = _rope_lane_freq()
    row = lambda t: t.reshape(1, -1)
    bf = lambda t: t.astype(BF16)
    xt = x.reshape(batch * seq, D_MODEL)
    for l in range(depth):
        w_in_l = bf(w_in[l])
        xt, q_sb, k_sb, v_sb, q_dl, k_dl, v_dl = _ffn_proj(
            xt, row(g_ffn1[l]), bf(w1_a[l]), bf(w3_a[l]), bf(w2_a[l]),
            row(g_mix[l]), w_in_l, lane_freq, seq)
        o_sb = _sb_attention(q_sb, k_sb, v_sb, batch, seq)
        o_dl = _dil_attention(q_dl, k_dl, v_dl, batch, seq)
        xt = _mix_ffn(xt, row(g_mix[l]), w_in_l, row(b_gate[l]), o_sb, o_dl,
                      bf(w_sb_out[l]), bf(w_dil_out[l]), bf(w_o[l]),
                      row(g_ffn2[l]), bf(w1_b[l]), bf(w3_b[l]), bf(w2_b[l]),
                      row(g_final) if l == depth - 1 else None)
    return xt.reshape(batch, seq, D_MODEL)
```
